```python
import jax, jax.numpy as jnp
from jax import lax
import numpy as np

D_MODEL = 1024
BATCH = 8
SEQ = 2048
DEPTH = 1
DEC_BATCH = 32
DEC_SEQ = 64
PAST_LEN = 4096

CHUNK = 64
RET_HEADS = 4
RET_DK = 128
RET_DV = 256
RET_QK = RET_HEADS * RET_DK
RET_V = RET_HEADS * RET_DV
CONV_DIM = D_MODEL
CONV_WIDTH = 31
CONV_HIST = CONV_WIDTH - 1
N_GROUPS = 4
EXPERTS_PER_GROUP = 8
TOP_K_INNER = 2
EXPERT_FF = 512
PLE_DIM = 256
LN_EPS = 1e-5
ROPE_BASE = 10000.0
DN_ALPHA = float((2 * DEPTH) ** 0.25)
DN_BETA = float((8 * DEPTH) ** -0.25)
SPLITS = [RET_QK, 2 * RET_QK, 2 * RET_QK + RET_V, 2 * RET_QK + 2 * RET_V,
          2 * RET_QK + 2 * RET_V + 2 * CONV_DIM, 2 * RET_QK + 2 * RET_V + 2 * CONV_DIM + D_MODEL]
IN_COLS = 2 * RET_QK + 2 * RET_V + 2 * CONV_DIM + 2 * D_MODEL

kernel_name = 'retention_conformer_hmoe_stream_step'


def layer_norm(x, g, b):
    xf = x.astype(jnp.float32)
    mu = jnp.mean(xf, axis=-1, keepdims=True)
    var = jnp.mean(jnp.square(xf - mu), axis=-1, keepdims=True)
    return ((xf - mu) * lax.rsqrt(var + LN_EPS) * g.astype(jnp.float32) + b.astype(jnp.float32)).astype(x.dtype)


def log_decay():
    return jnp.log1p(-jnp.exp2(-5.0 - jnp.arange(RET_HEADS, dtype=jnp.float32)))


def rotary(x, pos):
    half = RET_DK // 2
    inv = ROPE_BASE ** (-jnp.arange(half, dtype=jnp.float32) / half)
    ang = pos.astype(jnp.float32)[:, None] * inv[None, :]
    cos = jnp.cos(ang)[:, None, :]
    sin = jnp.sin(ang)[:, None, :]
    x1 = x[..., :half].astype(jnp.float32)
    x2 = x[..., half:].astype(jnp.float32)
    return jnp.concatenate([x1 * cos - x2 * sin, x1 * sin + x2 * cos], axis=-1).astype(x.dtype)


def in_projection(h, pos, w_in):
    B, T, _ = h.shape
    z = h @ w_in
    q, k, v, g, glu, ga, gb = jnp.split(z, SPLITS, axis=-1)
    q = rotary(q.reshape(B, T, RET_HEADS, RET_DK), pos)
    k = rotary(k.reshape(B, T, RET_HEADS, RET_DK), pos) * (RET_DK ** -0.5)
    v = v.reshape(B, T, RET_HEADS, RET_DV)
    return q, k, v, g, glu, ga, gb


def chunk_kv(k, v):
    L = k.shape[-3]
    idx = jnp.arange(L, dtype=jnp.float32)
    zeta = jnp.exp(log_decay()[:, None] * (L - 1.0 - idx)[None, :])
    return jnp.einsum('...lhk,...lhv,hl->...hkv', k, v, zeta)


def chunk_out(q, k, v, r_prev):
    L = q.shape[-3]
    lg = log_decay()
    idx = jnp.arange(L, dtype=jnp.float32)
    dmask = jnp.exp(lg[:, None, None] * jnp.abs(idx[:, None] - idx[None, :]))
    xi = jnp.exp(lg[:, None] * (idx + 1.0)[None, :])
    s = jnp.einsum('...lhk,...mhk->...hlm', q, k) * dmask
    inner = jnp.einsum('...hlm,...mhv->...lhv', s, v)
    cross = jnp.einsum('...lhk,...hkv->...lhv', q, r_prev) * xi.T[:, :, None]
    return inner + cross


def retention_out(o, g, w_ret_out):
    B, T = o.shape[0], o.shape[1]
    of = o.astype(jnp.float32)
    mu = jnp.mean(of, axis=-1, keepdims=True)
    var = jnp.mean(jnp.square(of - mu), axis=-1, keepdims=True)
    on = ((of - mu) * lax.rsqrt(var + LN_EPS)).reshape(B, T, RET_V).astype(g.dtype)
    return (on * jax.nn.silu(g)) @ w_ret_out


def glu_gate(glu):
    return glu[..., :CONV_DIM] * jax.nn.sigmoid(glu[..., CONV_DIM:])


def conv_out(u_pad, conv_w, conv_b, conv_ln_g, conv_ln_b, w_conv_out):
    c = lax.conv_general_dilated(u_pad, conv_w.astype(u_pad.dtype)[:, None, :], window_strides=(1,),
                                 padding='VALID', dimension_numbers=('NWC', 'WIO', 'NWC'),
                                 feature_group_count=CONV_DIM)
    c = jax.nn.silu(layer_norm(c + conv_b, conv_ln_g, conv_ln_b))
    return c @ w_conv_out


def merge(y_ret, y_conv, ga, gb, w_out):
    return (jax.nn.sigmoid(ga) * y_ret + jax.nn.sigmoid(gb) * y_conv) @ w_out


def token_mix_prompt(h, pos, w_in, w_ret_out, conv_w, conv_b, conv_ln_g, conv_ln_b, w_conv_out, w_out):
    B, T, _ = h.shape
    nc = T // CHUNK
    q, k, v, g, glu, ga, gb = in_projection(h, pos, w_in)
    qc = q.reshape(B, nc, CHUNK, RET_HEADS, RET_DK)
    kc = k.reshape(B, nc, CHUNK, RET_HEADS, RET_DK)
    vc = v.reshape(B, nc, CHUNK, RET_HEADS, RET_DV)
    kv = chunk_kv(kc, vc)
    chunk_decay = jnp.exp(log_decay() * CHUNK)[:, None, None]

    def step(r, kv_n):
        return chunk_decay * r + kv_n, r

    r_last, r_prev = lax.scan(step, jnp.zeros_like(kv[:, 0]), jnp.moveaxis(kv, 1, 0))
    o = chunk_out(qc, kc, vc, jnp.moveaxis(r_prev, 0, 1)).reshape(B, T, RET_HEADS, RET_DV)
    y_ret = retention_out(o, g, w_ret_out)
    u = glu_gate(glu)
    u_pad = jnp.pad(u, ((0, 0), (CONV_HIST, 0), (0, 0)))
    y_conv = conv_out(u_pad, conv_w, conv_b, conv_ln_g, conv_ln_b, w_conv_out)
    a = merge(y_ret, y_conv, ga, gb, w_out)
    return a, r_last.astype(h.dtype), u[:, T - CONV_HIST:]


def token_mix_sample(h, pos, r_state, c_state, w_in, w_ret_out, conv_w, conv_b, conv_ln_g, conv_ln_b, w_conv_out, w_out):
    L = h.shape[1]
    q, k, v, g, glu, ga, gb = in_projection(h, pos, w_in)
    o = chunk_out(q, k, v, r_state)
    r_new = jnp.exp(log_decay() * L)[:, None, None] * r_state + chunk_kv(k, v)
    y_ret = retention_out(o, g, w_ret_out)
    u = glu_gate(glu)
    u_pad = jnp.concatenate([c_state.astype(u.dtype), u], axis=1)
    y_conv = conv_out(u_pad, conv_w, conv_b, conv_ln_g, conv_ln_b, w_conv_out)
    a = merge(y_ret, y_conv, ga, gb, w_out)
    return a, r_new.astype(r_state.dtype), u_pad[:, u_pad.shape[1] - CONV_HIST:]


def hier_moe(h, w_rg, b_rg, w_re, b_re, w_eg, w_eu, w_ed):
    hf = h.astype(jnp.float32)
    gprob = jax.nn.softmax(hf @ w_rg.astype(jnp.float32) + b_rg.astype(jnp.float32), axis=-1)
    gp, gi = lax.top_k(gprob, 1)
    elog = jnp.einsum('nd,gde->nge', hf, w_re.astype(jnp.float32)) + b_re.astype(jnp.float32)
    elog = jnp.take_along_axis(elog, gi[:, :, None], axis=1)[:, 0]
    ev, ei = lax.top_k(elog, TOP_K_INNER)
    ew = jax.nn.softmax(ev, axis=-1) * gp
    inner = jnp.sum(jax.nn.one_hot(ei, EXPERTS_PER_GROUP, dtype=jnp.float32) * ew[..., None], axis=1)
    combine = (jax.nn.one_hot(gi[:, 0], N_GROUPS, dtype=jnp.float32)[:, :, None] * inner[:, None, :]).astype(h.dtype)
    out = jnp.zeros_like(h)
    for grp in range(N_GROUPS):
        act = jax.nn.silu(jnp.einsum('nd,edf->nef', h, w_eg[grp])) * jnp.einsum('nd,edf->nef', h, w_eu[grp])
        out = out + jnp.einsum('nef,efd->nd', act * combine[:, grp, :, None], w_ed[grp])
    return out


def channel_block(h, p, w_rg, b_rg, w_re, b_re, w_eg, w_eu, w_ed, w_ple_proj, w_ple_gate, ln2_g, ln2_b):
    B, T, D = h.shape
    m = hier_moe(h.reshape(B * T, D), w_rg, b_rg, w_re, b_re, w_eg, w_eu, w_ed).reshape(B, T, D)
    pe = (p.astype(h.dtype) @ w_ple_proj) * jax.nn.sigmoid(h @ w_ple_gate)
    return layer_norm(DN_ALPHA * h + m + pe, ln2_g, ln2_b)


def setup_inputs(seed: int = 0) -> dict:
    key = jax.random.key(seed)
    ks = jax.random.split(key, 32)

    def nrm(k, shape, scale):
        return jax.random.normal(k, shape, jnp.float32) * scale

    col_scale = np.ones((IN_COLS,), np.float32)
    col_scale[2 * RET_QK:2 * RET_QK + RET_V] = DN_BETA
    w_in = nrm(ks[6], (DEPTH, D_MODEL, IN_COLS), D_MODEL ** -0.5) * jnp.asarray(col_scale)
    return {
        'x_prompt': nrm(ks[0], (BATCH, SEQ, D_MODEL), 1.0),
        'x_sample': nrm(ks[1], (DEC_BATCH, DEC_SEQ, D_MODEL), 1.0),
        'p_prompt': nrm(ks[2], (DEPTH, BATCH, SEQ, PLE_DIM), 1.0),
        'p_sample': nrm(ks[3], (DEPTH, DEC_BATCH, DEC_SEQ, PLE_DIM), 1.0),
        'state_ret': nrm(ks[4], (DEPTH, DEC_BATCH, RET_HEADS, RET_DK, RET_DV), 0.3),
        'state_conv': nrm(ks[5], (DEPTH, DEC_BATCH, CONV_HIST, CONV_DIM), 0.5),
        'ln_in_g': 1.0 + nrm(ks[7], (D_MODEL,), 0.02),
        'ln_in_b': nrm(ks[8], (D_MODEL,), 0.02),
        'w_in': w_in,
        'w_ret_out': nrm(ks[9], (DEPTH, RET_V, D_MODEL), DN_BETA * RET_V ** -0.5),
        'conv_w': nrm(ks[10], (DEPTH, CONV_WIDTH, CONV_DIM), CONV_WIDTH ** -0.5),
        'conv_b': nrm(ks[11], (DEPTH, CONV_DIM), 0.02),
        'conv_ln_g': 1.0 + nrm(ks[12], (DEPTH, CONV_DIM), 0.02),
        'conv_ln_b': nrm(ks[13], (DEPTH, CONV_DIM), 0.02),
        'w_conv_out': nrm(ks[14], (DEPTH, CONV_DIM, D_MODEL), DN_BETA * CONV_DIM ** -0.5),
        'w_out': nrm(ks[15], (DEPTH, D_MODEL, D_MODEL), DN_BETA * D_MODEL ** -0.5),
        'ln1_g': 1.0 + nrm(ks[16], (DEPTH, D_MODEL), 0.02),
        'ln1_b': nrm(ks[17], (DEPTH, D_MODEL), 0.02),
        'w_route_g': nrm(ks[18], (DEPTH, D_MODEL, N_GROUPS), D_MODEL ** -0.5),
        'b_route_g': nrm(ks[19], (DEPTH, N_GROUPS), 0.01),
        'w_route_e': nrm(ks[20], (DEPTH, N_GROUPS, D_MODEL, EXPERTS_PER_GROUP), D_MODEL ** -0.5),
        'b_route_e': nrm(ks[21], (DEPTH, N_GROUPS, EXPERTS_PER_GROUP), 0.01),
        'w_exp_gate': nrm(ks[22], (DEPTH, N_GROUPS, EXPERTS_PER_GROUP, D_MODEL, EXPERT_FF), D_MODEL ** -0.5),
        'w_exp_up': nrm(ks[23], (DEPTH, N_GROUPS, EXPERTS_PER_GROUP, D_MODEL, EXPERT_FF), DN_BETA * D_MODEL ** -0.5),
        'w_exp_down': nrm(ks[24], (DEPTH, N_GROUPS, EXPERTS_PER_GROUP, EXPERT_FF, D_MODEL), DN_BETA * EXPERT_FF ** -0.5),
        'w_ple_proj': nrm(ks[25], (DEPTH, PLE_DIM, D_MODEL), DN_BETA * PLE_DIM ** -0.5),
        'w_ple_gate': nrm(ks[26], (DEPTH, D_MODEL, D_MODEL), D_MODEL ** -0.5),
        'ln2_g': 1.0 + nrm(ks[27], (DEPTH, D_MODEL), 0.02),
        'ln2_b': nrm(ks[28], (DEPTH, D_MODEL), 0.02),
    }


def reference(x_prompt, x_sample, p_prompt, p_sample, state_ret, state_conv,
              ln_in_g, ln_in_b, w_in, w_ret_out, conv_w, conv_b, conv_ln_g, conv_ln_b, w_conv_out, w_out,
              ln1_g, ln1_b, w_route_g, b_route_g, w_route_e, b_route_e, w_exp_gate, w_exp_up, w_exp_down,
              w_ple_proj, w_ple_gate, ln2_g, ln2_b):
    pos_p = jnp.arange(x_prompt.shape[1], dtype=jnp.int32)
    pos_s = PAST_LEN + jnp.arange(x_sample.shape[1], dtype=jnp.int32)
    hp = layer_norm(x_prompt, ln_in_g, ln_in_b)
    hs = layer_norm(x_sample, ln_in_g, ln_in_b)
    ret_p, conv_p, ret_s, conv_s = [], [], [], []
    for i in range(DEPTH):
        a_p, r_p, c_p = token_mix_prompt(hp, pos_p, w_in[i], w_ret_out[i], conv_w[i], conv_b[i],
                                         conv_ln_g[i], conv_ln_b[i], w_conv_out[i], w_out[i])
        a_s, r_s, c_s = token_mix_sample(hs, pos_s, state_ret[i], state_conv[i], w_in[i], w_ret_out[i], conv_w[i],
                                         conv_b[i], conv_ln_g[i], conv_ln_b[i], w_conv_out[i], w_out[i])
        ret_p.append(r_p)
        conv_p.append(c_p)
        ret_s.append(r_s)
        conv_s.append(c_s)
        hp = layer_norm(DN_ALPHA * hp + a_p, ln1_g[i], ln1_b[i])
        hs = layer_norm(DN_ALPHA * hs + a_s, ln1_g[i], ln1_b[i])
        hp = channel_block(hp, p_prompt[i], w_route_g[i], b_route_g[i], w_route_e[i], b_route_e[i], w_exp_gate[i],
                           w_exp_up[i], w_exp_down[i], w_ple_proj[i], w_ple_gate[i], ln2_g[i], ln2_b[i])
        hs = channel_block(hs, p_sample[i], w_route_g[i], b_route_g[i], w_route_e[i], b_route_e[i], w_exp_gate[i],
                           w_exp_up[i], w_exp_down[i], w_ple_proj[i], w_ple_gate[i], ln2_g[i], ln2_b[i])
    new_ret_prompt = jnp.stack(ret_p, axis=0)
    new_conv_prompt = jnp.stack(conv_p, axis=0)
    new_ret_sample = jnp.stack(ret_s, axis=0)
    new_conv_sample = jnp.stack(conv_s, axis=0)
    return (hp, hs, new_ret_prompt, new_conv_prompt, new_ret_sample, new_conv_sample)
```

```python
import functools
import math

import jax
import jax.numpy as jnp
import numpy as np
from jax import lax
from jax.experimental import pallas as pl
from jax.experimental.pallas import tpu as pltpu

D_MODEL = 1024
RET_HEADS = 4
RET_DK = 128
RET_DV = 256
RET_QK = RET_HEADS * RET_DK
RET_V = RET_HEADS * RET_DV
CHUNK = 64
CONV_WIDTH = 31
CONV_HIST = CONV_WIDTH - 1
HIST_ROWS = 32
HIST_PAD = HIST_ROWS - CONV_HIST
N_GROUPS = 4
EXPERTS_PER_GROUP = 8
N_EXPERTS = N_GROUPS * EXPERTS_PER_GROUP
EXPERT_FF = 512
PLE_DIM = 256
PAST_LEN = 4096
LN_EPS = 1e-5
ROPE_BASE = 10000.0
DEPTH = 1
DN_ALPHA = float((2 * DEPTH) ** 0.25)

LANES = 128
SUBLANES = 8
TOKEN_TILE = 256
EXPERT_TILE = 256
DISPATCH_TILE = 512
CONV_ROW_BLOCK = 64
VMEM_LIMIT = 56 * 1024 * 1024

C_Q = 0
C_K = RET_QK
C_V = 2 * RET_QK
C_G = C_V + RET_V
C_GLU_A = C_G + RET_V
C_GLU_B = C_GLU_A + D_MODEL
C_GA = C_GLU_B + D_MODEL
C_GB = C_GA + D_MODEL

ROUTE_COLS = N_GROUPS + N_EXPERTS
NEG_BIG = -1e30


def _sigmoid(x):
    return 1.0 / (1.0 + jnp.exp(-x))


def _layer_norm(x, g, b):
    mu = jnp.mean(x, axis=-1, keepdims=True)
    xc = x - mu
    var = jnp.mean(xc * xc, axis=-1, keepdims=True)
    return xc * lax.rsqrt(var + LN_EPS) * g + b


def _dot(a, b):
    return jnp.dot(a, b, preferred_element_type=jnp.float32)


def _mix_kernel(x_ref, p_ref, rinit_ref, cinit_ref, cq_ref, sq_ref, ck_ref, sk_ref,
                xi_ref, zeta_ref, mask_ref,
                ln_in_g_ref, ln_in_b_ref, w_in_ref, w_ret_out_ref, conv_w_ref, conv_b_ref,
                conv_ln_g_ref, conv_ln_b_ref, w_conv_out_ref, w_out_ref, ln1_g_ref, ln1_b_ref,
                w_route_ref, b_route_ref, w_ple_proj_ref, w_ple_gate_ref,
                h1_ref, base_ref, route_ref, rnew_ref, cnew_ref,
                r_scr, ubuf, c_scr,
                *, n_seq, seq_rows, state_decay):
    t = pl.program_id(1)
    last_t = pl.num_programs(1) - 1
    bf16 = jnp.bfloat16

    @pl.when(t == 0)
    def _():
        r_scr[...] = rinit_ref[...]
        ubuf[:, 0:HIST_ROWS, :] = cinit_ref[...]

    h = _layer_norm(x_ref[...], ln_in_g_ref[...], ln_in_b_ref[...])
    hb = h.astype(bf16)

    q = _dot(hb, w_in_ref[:, C_Q:C_Q + RET_QK])
    k = _dot(hb, w_in_ref[:, C_K:C_K + RET_QK])
    v = _dot(hb, w_in_ref[:, C_V:C_V + RET_V])
    g = _dot(hb, w_in_ref[:, C_G:C_G + RET_V])
    g = g * _sigmoid(g)
    cq, sq, ck, sk = cq_ref[...], sq_ref[...], ck_ref[...], sk_ref[...]

    y_ret = None
    for hd in range(RET_HEADS):
        qh = q[:, hd * RET_DK:(hd + 1) * RET_DK]
        kh = k[:, hd * RET_DK:(hd + 1) * RET_DK]
        qr = qh * cq + pltpu.roll(qh, RET_DK // 2, 1) * sq
        kr = kh * ck + pltpu.roll(kh, RET_DK // 2, 1) * sk
        vb = v[:, hd * RET_DV:(hd + 1) * RET_DV].astype(bf16)
        s = lax.dot_general(qr.astype(bf16), kr.astype(bf16), (((1,), (1,)), ((), ())),
                            preferred_element_type=jnp.float32)
        inner = _dot((s * mask_ref[hd]).astype(bf16), vb)
        qx = (qr * xi_ref[hd]).astype(bf16)
        kz = (kr * zeta_ref[hd]).astype(bf16)
        o_parts = []
        for sq_i in range(n_seq):
            rows = slice(sq_i * seq_rows, (sq_i + 1) * seq_rows)
            r_old = r_scr[sq_i, hd]
            o_parts.append(inner[rows] + _dot(qx[rows], r_old.astype(bf16)))
            kv = lax.dot_general(kz[rows], vb[rows], (((0,), (0,)), ((), ())),
                                 preferred_element_type=jnp.float32)
            r_scr[sq_i, hd] = state_decay[hd] * r_old + kv
        o = o_parts[0] if n_seq == 1 else jnp.concatenate(o_parts, axis=0)
        on = _layer_norm(o, 1.0, 0.0)
        gated = (on * g[:, hd * RET_DV:(hd + 1) * RET_DV]).astype(bf16)
        part = _dot(gated, w_ret_out_ref[hd * RET_DV:(hd + 1) * RET_DV, :])
        y_ret = part if y_ret is None else y_ret + part

    u = _dot(hb, w_in_ref[:, C_GLU_A:C_GLU_A + D_MODEL]) * _sigmoid(
        _dot(hb, w_in_ref[:, C_GLU_B:C_GLU_B + D_MODEL]))
    for sq_i in range(n_seq):
        ubuf[sq_i, HIST_ROWS:HIST_ROWS + seq_rows, :] = u[sq_i * seq_rows:(sq_i + 1) * seq_rows]

    row_block = min(CONV_ROW_BLOCK, seq_rows)

    def conv_lane_chunk(j, carry):
        lane = pl.ds(pl.multiple_of(j * LANES, LANES), LANES)
        for sq_i in range(n_seq):
            for rb in range(seq_rows // row_block):
                r0 = rb * row_block
                acc = jnp.zeros((row_block, LANES), jnp.float32)
                for sub in range(SUBLANES):
                    taps = [(a, SUBLANES * a + sub - HIST_PAD) for a in range(HIST_ROWS // SUBLANES + 1)]
                    taps = [(a, kk) for a, kk in taps if 0 <= kk < CONV_WIDTH]
                    span = row_block + SUBLANES * max(a for a, _ in taps)
                    shifted = ubuf[sq_i, r0 + sub:r0 + sub + span, lane]
                    for a, kk in taps:
                        acc = acc + conv_w_ref[kk:kk + 1, lane] * shifted[SUBLANES * a:SUBLANES * a + row_block]
                c_scr[sq_i * seq_rows + r0:sq_i * seq_rows + r0 + row_block, lane] = acc
        return carry

    lax.fori_loop(0, D_MODEL // LANES, conv_lane_chunk, 0)

    for sq_i in range(n_seq):
        ubuf[sq_i, 0:HIST_ROWS, :] = ubuf[sq_i, seq_rows:seq_rows + HIST_ROWS, :]

    @pl.when(t == last_t)
    def _():
        rnew_ref[...] = r_scr[...]
        cnew_ref[...] = ubuf[:, 0:HIST_ROWS, :]

    c = _layer_norm(c_scr[...] + conv_b_ref[...], conv_ln_g_ref[...], conv_ln_b_ref[...])
    c = c * _sigmoid(c)
    y_conv = _dot(c.astype(bf16), w_conv_out_ref[...])

    ga = _sigmoid(_dot(hb, w_in_ref[:, C_GA:C_GA + D_MODEL]))
    gb = _sigmoid(_dot(hb, w_in_ref[:, C_GB:C_GB + D_MODEL]))
    merged = (ga * y_ret + gb * y_conv).astype(bf16)
    h1 = _layer_norm(DN_ALPHA * h + _dot(merged, w_out_ref[...]), ln1_g_ref[...], ln1_b_ref[...])
    h1_ref[...] = h1
    h1b = h1.astype(bf16)

    pe = _dot(p_ref[...].astype(bf16), w_ple_proj_ref[...]) * _sigmoid(_dot(h1b, w_ple_gate_ref[...]))
    base_ref[...] = DN_ALPHA * h1 + pe

    logits = jnp.dot(h1, w_route_ref[...], preferred_element_type=jnp.float32,
                     precision=lax.Precision.HIGHEST) + b_route_ref[...]
    lane = lax.broadcasted_iota(jnp.int32, logits.shape, 1)
    lane_f = lane.astype(jnp.float32)
    is_group = lane < N_GROUPS
    gl = jnp.where(is_group, logits, NEG_BIG)
    gmax = jnp.max(gl, axis=-1, keepdims=True)
    gexp = jnp.where(is_group, jnp.exp(gl - gmax), 0.0)
    gprob = gexp / jnp.sum(gexp, axis=-1, keepdims=True)
    gp = jnp.max(gprob, axis=-1, keepdims=True)
    gi = jnp.min(jnp.where(is_group & (gprob == gp), lane_f, float(LANES)), axis=-1, keepdims=True)
    e_lo = N_GROUPS + EXPERTS_PER_GROUP * gi
    in_group = (lane_f >= e_lo) & (lane_f < e_lo + EXPERTS_PER_GROUP)
    el = jnp.where(in_group, logits, NEG_BIG)
    ev1 = jnp.max(el, axis=-1, keepdims=True)
    ei1 = jnp.min(jnp.where(in_group & (el == ev1), lane_f, float(LANES)), axis=-1, keepdims=True)
    rest = in_group & (lane_f != ei1)
    el2 = jnp.where(rest, logits, NEG_BIG)
    ev2 = jnp.max(el2, axis=-1, keepdims=True)
    ei2 = jnp.min(jnp.where(rest & (el2 == ev2), lane_f, float(LANES)), axis=-1, keepdims=True)
    x2 = jnp.exp(ev2 - ev1)
    w1 = gp / (1.0 + x2)
    w2 = w1 * x2
    route = jnp.where(lane == 0, ei1 - N_GROUPS, 0.0)
    route = jnp.where(lane == 1, ei2 - N_GROUPS, route)
    route = jnp.where(lane == 2, w1, route)
    route = jnp.where(lane == 3, w2, route)
    route_ref[...] = route


def _const_spec(shape):
    zeros = (0,) * len(shape)
    return pl.BlockSpec(shape, lambda b, t: zeros, pipeline_mode=pl.Buffered(1))


def _mix_call(x, p, rinit, cinit, tables, weights, *, n_seq, seq_rows, n_groups, n_steps, table_per_step):
    n_tok = x.shape[0]
    tt = n_seq * seq_rows
    assert tt == TOKEN_TILE and n_tok == n_groups * n_steps * tt
    log_decay = [math.log1p(-(2.0 ** (-5.0 - hd))) for hd in range(RET_HEADS)]
    state_decay = tuple(math.exp(lg * seq_rows) for lg in log_decay)
    cq, sq, ck, sk, xi, zeta, mask = tables

    tok_spec = lambda cols: pl.BlockSpec((tt, cols), lambda b, t: (b * n_steps + t, 0))
    rope_spec = pl.BlockSpec((tt, LANES), (lambda b, t: (t, 0)) if table_per_step else (lambda b, t: (0, 0)))
    state_r_spec = pl.BlockSpec((n_seq, RET_HEADS, RET_DK, RET_DV), lambda b, t: (b, 0, 0, 0))
    state_c_spec = pl.BlockSpec((n_seq, HIST_ROWS, D_MODEL), lambda b, t: (b, 0, 0))

    in_specs = [tok_spec(D_MODEL), tok_spec(PLE_DIM), state_r_spec, state_c_spec,
                rope_spec, rope_spec, rope_spec, rope_spec,
                _const_spec(xi.shape), _const_spec(zeta.shape), _const_spec(mask.shape)]
    in_specs += [_const_spec(w.shape) for w in weights]
    out_shape = (jax.ShapeDtypeStruct((n_tok, D_MODEL), jnp.float32),
                 jax.ShapeDtypeStruct((n_tok, D_MODEL), jnp.float32),
                 jax.ShapeDtypeStruct((n_tok, LANES), jnp.float32),
                 jax.ShapeDtypeStruct(rinit.shape, jnp.float32),
                 jax.ShapeDtypeStruct(cinit.shape, jnp.float32))
    out_specs = (tok_spec(D_MODEL), tok_spec(D_MODEL), tok_spec(LANES), state_r_spec, state_c_spec)
    body = functools.partial(_mix_kernel, n_seq=n_seq, seq_rows=seq_rows, state_decay=state_decay)
    return pl.pallas_call(
        body,
        grid=(n_groups, n_steps),
        in_specs=in_specs,
        out_specs=out_specs,
        out_shape=out_shape,
        scratch_shapes=[pltpu.VMEM((n_seq, RET_HEADS, RET_DK, RET_DV), jnp.float32),
                        pltpu.VMEM((n_seq, HIST_ROWS + seq_rows, D_MODEL), jnp.float32),
                        pltpu.VMEM((tt, D_MODEL), jnp.float32)],
        compiler_params=pltpu.CompilerParams(dimension_semantics=("arbitrary", "arbitrary"),
                                             vmem_limit_bytes=VMEM_LIMIT),
        name="mix",
    )(x, p, rinit, cinit, cq, sq, ck, sk, xi, zeta, mask, *weights)


def _mix_tables(positions, n_seq, seq_rows):
    half = RET_DK // 2
    inv = ROPE_BASE ** (-jnp.arange(half, dtype=jnp.float32) / half)
    ang = positions.astype(jnp.float32)[:, None] * inv[None, :]
    cos, sin = jnp.cos(ang), jnp.sin(ang)
    cq = jnp.concatenate([cos, cos], axis=1)
    sq = jnp.concatenate([-sin, sin], axis=1)
    scale = RET_DK ** -0.5
    lg = jnp.log1p(-jnp.exp2(-5.0 - jnp.arange(RET_HEADS, dtype=jnp.float32)))
    tt = n_seq * seq_rows
    i = jnp.arange(tt)
    loc = (i % seq_rows).astype(jnp.float32)
    xi = jnp.exp(lg[:, None] * (loc + 1.0)[None, :])
    zeta = jnp.exp(lg[:, None] * (seq_rows - 1.0 - loc)[None, :])
    xi = jnp.broadcast_to(xi[:, :, None], (RET_HEADS, tt, LANES))
    zeta = jnp.broadcast_to(zeta[:, :, None], (RET_HEADS, tt, LANES))
    same_seq = (i[:, None] // seq_rows) == (i[None, :] // seq_rows)
    visible = same_seq & ((i[None, :] // CHUNK) <= (i[:, None] // CHUNK))
    dist = jnp.abs(i[:, None] - i[None, :]).astype(jnp.float32)
    mask = jnp.where(visible[None], jnp.exp(lg[:, None, None] * dist[None]), 0.0)
    return cq, sq, cq * scale, sq * scale, xi, zeta, mask


def _dispatch_kernel(pos_ref, h1_hbm, xs_in_hbm, xs_hbm, sem):
    del xs_in_hbm
    step = pl.program_id(0)
    tok0 = step * DISPATCH_TILE

    def row_copy(src_row, dst_row):
        return pltpu.make_async_copy(h1_hbm.at[pl.ds(src_row, 1)], xs_hbm.at[pl.ds(dst_row, 1)], sem)

    def issue(i, carry):
        tok = tok0 + i
        row_copy(tok, pos_ref[2 * tok]).start()
        row_copy(tok, pos_ref[2 * tok + 1]).start()
        return carry

    lax.fori_loop(0, DISPATCH_TILE, issue, 0)

    def drain(i, carry):
        row_copy(0, 0).wait()
        row_copy(0, 0).wait()
        return carry

    lax.fori_loop(0, DISPATCH_TILE, drain, 0)


def _dispatch_call(pos, h1, xs_init):
    n_tok = h1.shape[0]
    assert n_tok % DISPATCH_TILE == 0
    grid_spec = pltpu.PrefetchScalarGridSpec(
        num_scalar_prefetch=1,
        grid=(n_tok // DISPATCH_TILE,),
        in_specs=[pl.BlockSpec(memory_space=pl.ANY), pl.BlockSpec(memory_space=pl.ANY)],
        out_specs=pl.BlockSpec(memory_space=pl.ANY),
        scratch_shapes=[pltpu.SemaphoreType.DMA(())],
    )
    return pl.pallas_call(
        _dispatch_kernel,
        grid_spec=grid_spec,
        out_shape=jax.ShapeDtypeStruct(xs_init.shape, xs_init.dtype),
        input_output_aliases={2: 0},
        compiler_params=pltpu.CompilerParams(dimension_semantics=("arbitrary",)),
        name="dispatch",
    )(pos, h1, xs_init)


def _expert_kernel(tile_expert_ref, n_used_ref, xs_ref, wg_ref, wu_ref, wd_ref, ys_ref, wg_b, wu_b, wd_b):
    i = pl.program_id(0)
    bf16 = jnp.bfloat16
    prev = tile_expert_ref[jnp.maximum(i - 1, 0)]
    new_expert = (i == 0) | (tile_expert_ref[i] != prev)

    @pl.when(new_expert)
    def _():
        wg_b[...] = wg_ref[0].astype(bf16)
        wu_b[...] = wu_ref[0].astype(bf16)
        wd_b[...] = wd_ref[0].astype(bf16)

    @pl.when(i < n_used_ref[0])
    def _():
        x = xs_ref[...].astype(bf16)
        gate = _dot(x, wg_b[...])
        up = _dot(x, wu_b[...])
        act = (gate * _sigmoid(gate) * up).astype(bf16)
        ys_ref[...] = _dot(act, wd_b[...])

    @pl.when(i >= n_used_ref[0])
    def _():
        ys_ref[...] = jnp.zeros_like(ys_ref)


def _expert_call(tile_expert, n_used, xs, wg, wu, wd):
    n_tiles = xs.shape[0] // EXPERT_TILE
    row_map = lambda i, te, nu: (jnp.minimum(i, nu[0] - 1), 0)
    w_map = lambda i, te, nu: (te[i], 0, 0)
    grid_spec = pltpu.PrefetchScalarGridSpec(
        num_scalar_prefetch=2,
        grid=(n_tiles,),
        in_specs=[pl.BlockSpec((EXPERT_TILE, D_MODEL), row_map),
                  pl.BlockSpec((1, D_MODEL, EXPERT_FF), w_map),
                  pl.BlockSpec((1, D_MODEL, EXPERT_FF), w_map),
                  pl.BlockSpec((1, EXPERT_FF, D_MODEL), w_map)],
        out_specs=pl.BlockSpec((EXPERT_TILE, D_MODEL), lambda i, te, nu: (i, 0)),
        scratch_shapes=[pltpu.VMEM((D_MODEL, EXPERT_FF), jnp.bfloat16),
                        pltpu.VMEM((D_MODEL, EXPERT_FF), jnp.bfloat16),
                        pltpu.VMEM((EXPERT_FF, D_MODEL), jnp.bfloat16)],
    )
    return pl.pallas_call(
        _expert_kernel,
        grid_spec=grid_spec,
        out_shape=jax.ShapeDtypeStruct(xs.shape, jnp.float32),
        compiler_params=pltpu.CompilerParams(dimension_semantics=("arbitrary",),
                                             vmem_limit_bytes=VMEM_LIMIT),
        name="experts",
    )(tile_expert, n_used, xs, wg, wu, wd)


def _combine_kernel(pos_ref, base_ref, route_ref, ys_hbm, ln2_g_ref, ln2_b_ref, out_ref, ybuf, sem):
    step = pl.program_id(0)
    tok0 = step * TOKEN_TILE

    def row_copy(src_row, slot, dst_row):
        return pltpu.make_async_copy(ys_hbm.at[pl.ds(src_row, 1)], ybuf.at[slot, pl.ds(dst_row, 1)], sem)

    def issue(i, carry):
        row_copy(pos_ref[2 * (tok0 + i)], 0, i).start()
        row_copy(pos_ref[2 * (tok0 + i) + 1], 1, i).start()
        return carry

    lax.fori_loop(0, TOKEN_TILE, issue, 0)

    def drain(i, carry):
        row_copy(0, 0, 0).wait()
        row_copy(0, 1, 0).wait()
        return carry

    lax.fori_loop(0, TOKEN_TILE, drain, 0)

    route = route_ref[...]
    w1 = route[:, 2:3]
    w2 = route[:, 3:4]
    total = base_ref[...] + w1 * ybuf[0] + w2 * ybuf[1]
    out_ref[...] = _layer_norm(total, ln2_g_ref[...], ln2_b_ref[...])


def _combine_call(pos, base, route, ys, ln2_g, ln2_b):
    n_tok = base.shape[0]
    tok_spec = lambda cols: pl.BlockSpec((TOKEN_TILE, cols), lambda i, pos: (i, 0))
    vec_spec = pl.BlockSpec((1, D_MODEL), lambda i, pos: (0, 0))
    grid_spec = pltpu.PrefetchScalarGridSpec(
        num_scalar_prefetch=1,
        grid=(n_tok // TOKEN_TILE,),
        in_specs=[tok_spec(D_MODEL), tok_spec(LANES), pl.BlockSpec(memory_space=pl.ANY), vec_spec, vec_spec],
        out_specs=tok_spec(D_MODEL),
        scratch_shapes=[pltpu.VMEM((2, TOKEN_TILE, D_MODEL), jnp.float32), pltpu.SemaphoreType.DMA(())],
    )
    return pl.pallas_call(
        _combine_kernel,
        grid_spec=grid_spec,
        out_shape=jax.ShapeDtypeStruct((n_tok, D_MODEL), jnp.float32),
        compiler_params=pltpu.CompilerParams(dimension_semantics=("arbitrary",)),
        name="combine",
    )(pos, base, route, ys, ln2_g, ln2_b)


def _dispatch_plan(route, n_tiles):
    experts = route[:, :2].astype(jnp.int32).reshape(-1)
    onehot = (experts[:, None] == jnp.arange(N_EXPERTS, dtype=jnp.int32)[None, :]).astype(jnp.int32)
    running = jnp.cumsum(onehot, axis=0)
    rank = jnp.sum(running * onehot, axis=1) - 1
    counts = running[-1]
    tiles_per = (counts + EXPERT_TILE - 1) // EXPERT_TILE
    tile_end = jnp.cumsum(tiles_per)
    row_start = (tile_end - tiles_per) * EXPERT_TILE
    pos = row_start[experts] + rank
    n_used = tile_end[-1]
    tile_ids = jnp.minimum(jnp.arange(n_tiles, dtype=jnp.int32), n_used - 1)
    tile_expert = jnp.sum((tile_ids[:, None] >= tile_end[None, :]).astype(jnp.int32), axis=1)
    return pos.astype(jnp.int32), tile_expert.astype(jnp.int32), n_used.reshape(1).astype(jnp.int32)


def kernel(x_prompt, x_sample, p_prompt, p_sample, state_ret, state_conv, ln_in_g, ln_in_b, w_in, w_ret_out,
           conv_w, conv_b, conv_ln_g, conv_ln_b, w_conv_out, w_out, ln1_g, ln1_b, w_route_g, b_route_g,
           w_route_e, b_route_e, w_exp_gate, w_exp_up, w_exp_down, w_ple_proj, w_ple_gate, ln2_g, ln2_b):
    assert w_in.shape[0] == DEPTH == 1
    bf16 = jnp.bfloat16
    batch, seq, _ = x_prompt.shape
    dec_batch, dec_seq, _ = x_sample.shape
    assert dec_seq == CHUNK and seq % TOKEN_TILE == 0 and TOKEN_TILE % CHUNK == 0
    row = lambda a: a.reshape(1, -1)

    w_route = jnp.concatenate(
        [w_route_g[0], jnp.transpose(w_route_e[0], (1, 0, 2)).reshape(D_MODEL, N_EXPERTS),
         jnp.zeros((D_MODEL, LANES - ROUTE_COLS), jnp.float32)], axis=1)
    b_route = jnp.concatenate([b_route_g[0], b_route_e[0].reshape(-1),
                               jnp.zeros((LANES - ROUTE_COLS,), jnp.float32)]).reshape(1, LANES)
    conv_w_pad = jnp.concatenate([conv_w[0], jnp.zeros((HIST_ROWS - CONV_WIDTH, D_MODEL), jnp.float32)], axis=0)
    weights = (row(ln_in_g), row(ln_in_b), w_in[0].astype(bf16), w_ret_out[0].astype(bf16), conv_w_pad,
               row(conv_b[0]), row(conv_ln_g[0]), row(conv_ln_b[0]), w_conv_out[0].astype(bf16),
               w_out[0].astype(bf16), row(ln1_g[0]), row(ln1_b[0]), w_route, b_route,
               w_ple_proj[0].astype(bf16), w_ple_gate[0].astype(bf16))

    tables_p = _mix_tables(jnp.arange(seq, dtype=jnp.int32), 1, TOKEN_TILE)
    rinit_p = jnp.zeros((batch, RET_HEADS, RET_DK, RET_DV), jnp.float32)
    cinit_p = jnp.zeros((batch, HIST_ROWS, D_MODEL), jnp.float32)
    h1_p, base_p, route_p, rnew_p, cnew_p = _mix_call(
        x_prompt.reshape(batch * seq, D_MODEL), p_prompt[0].reshape(batch * seq, PLE_DIM), rinit_p, cinit_p,
        tables_p, weights, n_seq=1, seq_rows=TOKEN_TILE, n_groups=batch, n_steps=seq // TOKEN_TILE,
        table_per_step=True)

    seqs_per_tile = TOKEN_TILE // dec_seq
    assert dec_batch % seqs_per_tile == 0
    pos_s = jnp.tile(PAST_LEN + jnp.arange(dec_seq, dtype=jnp.int32), seqs_per_tile)
    tables_s = _mix_tables(pos_s, seqs_per_tile, dec_seq)
    cinit_s = jnp.pad(state_conv[0], ((0, 0), (HIST_PAD, 0), (0, 0)))
    h1_s, base_s, route_s, rnew_s, cnew_s = _mix_call(
        x_sample.reshape(dec_batch * dec_seq, D_MODEL), p_sample[0].reshape(dec_batch * dec_seq, PLE_DIM),
        state_ret[0], cinit_s, tables_s, weights, n_seq=seqs_per_tile, seq_rows=dec_seq,
        n_groups=dec_batch // seqs_per_tile, n_steps=1, table_per_step=False)

    h1 = jnp.concatenate([h1_p, h1_s], axis=0)
    base = jnp.concatenate([base_p, base_s], axis=0)
    route = jnp.concatenate([route_p, route_s], axis=0)
    n_tok = h1.shape[0]

    n_tiles = (2 * n_tok) // EXPERT_TILE + N_EXPERTS
    pos, tile_expert, n_used = _dispatch_plan(route, n_tiles)
    xs = _dispatch_call(pos, h1, jnp.zeros((n_tiles * EXPERT_TILE, D_MODEL), jnp.float32))
    ys = _expert_call(tile_expert, n_used, xs,
                      w_exp_gate[0].reshape(N_EXPERTS, D_MODEL, EXPERT_FF),
                      w_exp_up[0].reshape(N_EXPERTS, D_MODEL, EXPERT_FF),
                      w_exp_down[0].reshape(N_EXPERTS, EXPERT_FF, D_MODEL))
    y = _combine_call(pos, base, route, ys, row(ln2_g[0]), row(ln2_b[0]))

    y_prompt = y[:batch * seq].reshape(batch, seq, D_MODEL)
    y_sample = y[batch * seq:].reshape(dec_batch, dec_seq, D_MODEL)
    return (y_prompt, y_sample, rnew_p[None], cnew_p[None, :, HIST_PAD:, :],
            rnew_s[None], cnew_s[None, :, HIST_PAD:, :])
```

```python
import functools
import math

import jax
import jax.numpy as jnp
from jax import lax
from jax.experimental import pallas as pl
from jax.experimental.pallas import tpu as pltpu

D_MODEL = 1024
RET_HEADS = 4
RET_DK = 128
RET_DV = 256
RET_QK = RET_HEADS * RET_DK
RET_V = RET_HEADS * RET_DV
CHUNK = 64
CONV_WIDTH = 31
CONV_HIST = CONV_WIDTH - 1
HIST_ROWS = 32
HIST_PAD = HIST_ROWS - CONV_HIST
N_GROUPS = 4
EXPERTS_PER_GROUP = 8
N_EXPERTS = N_GROUPS * EXPERTS_PER_GROUP
EXPERT_FF = 512
PLE_DIM = 256
PAST_LEN = 4096
LN_EPS = 1e-5
ROPE_BASE = 10000.0
DEPTH = 1
DN_ALPHA = float((2 * DEPTH) ** 0.25)

LANES = 128
SUBLANES = 8
TOKEN_TILE = 256
EXPERT_TILE = 256
CONV_ROW_BLOCK = 64
VMEM_LIMIT = 56 * 1024 * 1024

C_Q = 0
C_K = RET_QK
C_V = 2 * RET_QK
C_G = C_V + RET_V
C_GLU_A = C_G + RET_V
C_GLU_B = C_GLU_A + D_MODEL
C_GA = C_GLU_B + D_MODEL
C_GB = C_GA + D_MODEL

ROUTE_COLS = N_GROUPS + N_EXPERTS
NEG_BIG = -1e30


def _sigmoid(x):
    return 1.0 / (1.0 + jnp.exp(-x))


def _layer_norm(x, g, b):
    mu = jnp.mean(x, axis=-1, keepdims=True)
    xc = x - mu
    var = jnp.mean(xc * xc, axis=-1, keepdims=True)
    return xc * lax.rsqrt(var + LN_EPS) * g + b


def _dot(a, b):
    return jnp.dot(a, b, preferred_element_type=jnp.float32)


def _mix_kernel(x_ref, rinit_ref, cinit_ref, cq_ref, sq_ref, ck_ref, sk_ref,
                xi_ref, zeta_ref, mask_ref,
                ln_in_g_ref, ln_in_b_ref, w_in_ref, w_ret_out_ref, conv_w_ref, conv_b_ref,
                conv_ln_g_ref, conv_ln_b_ref, w_conv_out_ref, w_out_ref, ln1_g_ref, ln1_b_ref,
                w_route_ref, w_route_hi_ref, b_route_ref, h1_alias_ref, route_alias_ref,
                h1_ref, route_ref, rnew_ref, cnew_ref,
                r_scr, ubuf, c_scr,
                *, n_seq, seq_rows, state_decay):
    del h1_alias_ref, route_alias_ref
    t = pl.program_id(1)
    last_t = pl.num_programs(1) - 1
    bf16 = jnp.bfloat16

    @pl.when(t == 0)
    def _():
        r_scr[...] = rinit_ref[...]
        ubuf[:, 0:HIST_ROWS, :] = cinit_ref[...]

    h = _layer_norm(x_ref[...], ln_in_g_ref[...], ln_in_b_ref[...])
    hb = h.astype(bf16)

    q = _dot(hb, w_in_ref[:, C_Q:C_Q + RET_QK])
    k = _dot(hb, w_in_ref[:, C_K:C_K + RET_QK])
    v = _dot(hb, w_in_ref[:, C_V:C_V + RET_V])
    g = _dot(hb, w_in_ref[:, C_G:C_G + RET_V])
    g = g * _sigmoid(g)
    cq, sq, ck, sk = cq_ref[...], sq_ref[...], ck_ref[...], sk_ref[...]

    y_ret = None
    for hd in range(RET_HEADS):
        qh = q[:, hd * RET_DK:(hd + 1) * RET_DK]
        kh = k[:, hd * RET_DK:(hd + 1) * RET_DK]
        qr = qh * cq + pltpu.roll(qh, RET_DK // 2, 1) * sq
        kr = kh * ck + pltpu.roll(kh, RET_DK // 2, 1) * sk
        vb = v[:, hd * RET_DV:(hd + 1) * RET_DV].astype(bf16)
        s = lax.dot_general(qr.astype(bf16), kr.astype(bf16), (((1,), (1,)), ((), ())),
                            preferred_element_type=jnp.float32)
        inner = _dot((s * mask_ref[hd]).astype(bf16), vb)
        qx = (qr * xi_ref[hd]).astype(bf16)
        kz = (kr * zeta_ref[hd]).astype(bf16)
        o_parts = []
        for sq_i in range(n_seq):
            rows = slice(sq_i * seq_rows, (sq_i + 1) * seq_rows)
            r_old = r_scr[sq_i, hd]
            o_parts.append(inner[rows] + _dot(qx[rows], r_old.astype(bf16)))
            kv = lax.dot_general(kz[rows], vb[rows], (((0,), (0,)), ((), ())),
                                 preferred_element_type=jnp.float32)
            r_scr[sq_i, hd] = state_decay[hd] * r_old + kv
        o = o_parts[0] if n_seq == 1 else jnp.concatenate(o_parts, axis=0)
        on = _layer_norm(o, 1.0, 0.0)
        gated = (on * g[:, hd * RET_DV:(hd + 1) * RET_DV]).astype(bf16)
        part = _dot(gated, w_ret_out_ref[hd * RET_DV:(hd + 1) * RET_DV, :])
        y_ret = part if y_ret is None else y_ret + part

    u = _dot(hb, w_in_ref[:, C_GLU_A:C_GLU_A + D_MODEL]) * _sigmoid(
        _dot(hb, w_in_ref[:, C_GLU_B:C_GLU_B + D_MODEL]))
    for sq_i in range(n_seq):
        ubuf[sq_i, HIST_ROWS:HIST_ROWS + seq_rows, :] = u[sq_i * seq_rows:(sq_i + 1) * seq_rows]

    row_block = min(CONV_ROW_BLOCK, seq_rows)

    def conv_lane_chunk(j, carry):
        lane = pl.ds(pl.multiple_of(j * LANES, LANES), LANES)
        for sq_i in range(n_seq):
            for rb in range(seq_rows // row_block):
                r0 = rb * row_block
                acc = jnp.zeros((row_block, LANES), jnp.float32)
                for sub in range(SUBLANES):
                    taps = [(a, SUBLANES * a + sub - HIST_PAD) for a in range(HIST_ROWS // SUBLANES + 1)]
                    taps = [(a, kk) for a, kk in taps if 0 <= kk < CONV_WIDTH]
                    span = row_block + SUBLANES * max(a for a, _ in taps)
                    shifted = ubuf[sq_i, r0 + sub:r0 + sub + span, lane]
                    for a, kk in taps:
                        acc = acc + conv_w_ref[kk:kk + 1, lane] * shifted[SUBLANES * a:SUBLANES * a + row_block]
                c_scr[sq_i * seq_rows + r0:sq_i * seq_rows + r0 + row_block, lane] = acc
        return carry

    lax.fori_loop(0, D_MODEL // LANES, conv_lane_chunk, 0)

    for sq_i in range(n_seq):
        ubuf[sq_i, 0:HIST_ROWS, :] = ubuf[sq_i, seq_rows:seq_rows + HIST_ROWS, :]

    @pl.when(t == last_t)
    def _():
        rnew_ref[...] = r_scr[...]
        cnew_ref[...] = ubuf[:, 0:HIST_ROWS, :]

    c = _layer_norm(c_scr[...] + conv_b_ref[...], conv_ln_g_ref[...], conv_ln_b_ref[...])
    c = c * _sigmoid(c)
    y_conv = _dot(c.astype(bf16), w_conv_out_ref[...])

    ga = _sigmoid(_dot(hb, w_in_ref[:, C_GA:C_GA + D_MODEL]))
    gb = _sigmoid(_dot(hb, w_in_ref[:, C_GB:C_GB + D_MODEL]))
    merged = (ga * y_ret + gb * y_conv).astype(bf16)
    h1 = _layer_norm(DN_ALPHA * h + _dot(merged, w_out_ref[...]), ln1_g_ref[...], ln1_b_ref[...])
    h1_ref[...] = h1

    h1_hi = h1.astype(bf16)
    h1_lo = (h1 - h1_hi.astype(jnp.float32)).astype(bf16)
    both = _dot(h1_hi, w_route_ref[...])
    logits = both[:, :LANES] + both[:, LANES:] + _dot(h1_lo, w_route_hi_ref[...]) + b_route_ref[...]
    lane = lax.broadcasted_iota(jnp.int32, logits.shape, 1)
    lane_f = lane.astype(jnp.float32)
    is_group = lane < N_GROUPS
    gl = jnp.where(is_group, logits, NEG_BIG)
    gmax = jnp.max(gl, axis=-1, keepdims=True)
    gexp = jnp.where(is_group, jnp.exp(gl - gmax), 0.0)
    gprob = gexp / jnp.sum(gexp, axis=-1, keepdims=True)
    gp = jnp.max(gprob, axis=-1, keepdims=True)
    gi = jnp.min(jnp.where(is_group & (gprob == gp), lane_f, float(LANES)), axis=-1, keepdims=True)
    e_lo = N_GROUPS + EXPERTS_PER_GROUP * gi
    in_group = (lane_f >= e_lo) & (lane_f < e_lo + EXPERTS_PER_GROUP)
    el = jnp.where(in_group, logits, NEG_BIG)
    ev1 = jnp.max(el, axis=-1, keepdims=True)
    ei1 = jnp.min(jnp.where(in_group & (el == ev1), lane_f, float(LANES)), axis=-1, keepdims=True)
    rest = in_group & (lane_f != ei1)
    el2 = jnp.where(rest, logits, NEG_BIG)
    ev2 = jnp.max(el2, axis=-1, keepdims=True)
    ei2 = jnp.min(jnp.where(rest & (el2 == ev2), lane_f, float(LANES)), axis=-1, keepdims=True)
    x2 = jnp.exp(ev2 - ev1)
    w1 = gp / (1.0 + x2)
    w2 = w1 * x2
    route = jnp.where(lane == 0, ei1 - N_GROUPS, 0.0)
    route = jnp.where(lane == 1, ei2 - N_GROUPS, route)
    route = jnp.where(lane == 2, w1, route)
    route = jnp.where(lane == 3, w2, route)
    route_ref[...] = route


def _const_spec(shape):
    zeros = (0,) * len(shape)
    return pl.BlockSpec(shape, lambda b, t: zeros, pipeline_mode=pl.Buffered(1))


def _mix_call(x, rinit, cinit, tables, weights, h1_buf, route_buf, *, n_seq, seq_rows, n_groups, n_steps,
              table_per_step, tile_offset):
    tt = n_seq * seq_rows
    assert tt == TOKEN_TILE and x.shape[0] == n_groups * n_steps * tt
    log_decay = [math.log1p(-(2.0 ** (-5.0 - hd))) for hd in range(RET_HEADS)]
    state_decay = tuple(math.exp(lg * seq_rows) for lg in log_decay)
    cq, sq, ck, sk, xi, zeta, mask = tables

    in_tok_spec = pl.BlockSpec((tt, D_MODEL), lambda b, t: (b * n_steps + t, 0))
    out_tok_spec = lambda cols: pl.BlockSpec((tt, cols), lambda b, t: (tile_offset + b * n_steps + t, 0))
    rope_spec = pl.BlockSpec((tt, LANES), (lambda b, t: (t, 0)) if table_per_step else (lambda b, t: (0, 0)))
    state_r_spec = pl.BlockSpec((n_seq, RET_HEADS, RET_DK, RET_DV), lambda b, t: (b, 0, 0, 0))
    state_c_spec = pl.BlockSpec((n_seq, HIST_ROWS, D_MODEL), lambda b, t: (b, 0, 0))
    any_spec = pl.BlockSpec(memory_space=pl.ANY)

    in_specs = [in_tok_spec, state_r_spec, state_c_spec, rope_spec, rope_spec, rope_spec, rope_spec,
                _const_spec(xi.shape), _const_spec(zeta.shape), _const_spec(mask.shape)]
    in_specs += [_const_spec(w.shape) for w in weights]
    in_specs += [any_spec, any_spec]
    n_in = len(in_specs)
    out_shape = (jax.ShapeDtypeStruct(h1_buf.shape, jnp.float32),
                 jax.ShapeDtypeStruct(route_buf.shape, jnp.float32),
                 jax.ShapeDtypeStruct(rinit.shape, jnp.float32),
                 jax.ShapeDtypeStruct(cinit.shape, jnp.float32))
    out_specs = (out_tok_spec(D_MODEL), out_tok_spec(LANES), state_r_spec, state_c_spec)
    body = functools.partial(_mix_kernel, n_seq=n_seq, seq_rows=seq_rows, state_decay=state_decay)
    return pl.pallas_call(
        body,
        grid=(n_groups, n_steps),
        in_specs=in_specs,
        out_specs=out_specs,
        out_shape=out_shape,
        input_output_aliases={n_in - 2: 0, n_in - 1: 1},
        scratch_shapes=[pltpu.VMEM((n_seq, RET_HEADS, RET_DK, RET_DV), jnp.float32),
                        pltpu.VMEM((n_seq, HIST_ROWS + seq_rows, D_MODEL), jnp.float32),
                        pltpu.VMEM((tt, D_MODEL), jnp.float32)],
        compiler_params=pltpu.CompilerParams(dimension_semantics=("arbitrary", "arbitrary"),
                                             vmem_limit_bytes=VMEM_LIMIT),
        name="mix",
    )(x, rinit, cinit, cq, sq, ck, sk, xi, zeta, mask, *weights, h1_buf, route_buf)


def _mix_tables(positions, n_seq, seq_rows):
    half = RET_DK // 2
    inv = ROPE_BASE ** (-jnp.arange(half, dtype=jnp.float32) / half)
    ang = positions.astype(jnp.float32)[:, None] * inv[None, :]
    cos, sin = jnp.cos(ang), jnp.sin(ang)
    cq = jnp.concatenate([cos, cos], axis=1)
    sq = jnp.concatenate([-sin, sin], axis=1)
    scale = RET_DK ** -0.5
    lg = jnp.log1p(-jnp.exp2(-5.0 - jnp.arange(RET_HEADS, dtype=jnp.float32)))
    tt = n_seq * seq_rows
    i = jnp.arange(tt)
    loc = (i % seq_rows).astype(jnp.float32)
    xi = jnp.exp(lg[:, None] * (loc + 1.0)[None, :])
    zeta = jnp.exp(lg[:, None] * (seq_rows - 1.0 - loc)[None, :])
    xi = jnp.broadcast_to(xi[:, :, None], (RET_HEADS, tt, LANES))
    zeta = jnp.broadcast_to(zeta[:, :, None], (RET_HEADS, tt, LANES))
    same_seq = (i[:, None] // seq_rows) == (i[None, :] // seq_rows)
    visible = same_seq & ((i[None, :] // CHUNK) <= (i[:, None] // CHUNK))
    dist = jnp.abs(i[:, None] - i[None, :]).astype(jnp.float32)
    mask = jnp.where(visible[None], jnp.exp(lg[:, None, None] * dist[None]), 0.0)
    return cq, sq, cq * scale, sq * scale, xi, zeta, mask


def _expert_kernel(tile_expert_ref, n_used_ref, row_token_ref, h1_hbm, wg_ref, wu_ref, wd_ref, ys_ref,
                   xbuf, sems, wg_b, wu_b, wd_b):
    i = pl.program_id(0)
    n_used = n_used_ref[0]
    bf16 = jnp.bfloat16
    slot = lax.rem(i, 2)

    def start_gather(tile, dst_slot):
        def issue(r, carry):
            tok = row_token_ref[tile * EXPERT_TILE + r]
            pltpu.make_async_copy(h1_hbm.at[pl.ds(tok, 1)], xbuf.at[dst_slot, pl.ds(r, 1)],
                                  sems.at[dst_slot]).start()
            return carry
        lax.fori_loop(0, EXPERT_TILE, issue, 0, unroll=8)

    def wait_gather(dst_slot):
        pltpu.make_async_copy(h1_hbm.at[pl.ds(0, EXPERT_TILE)], xbuf.at[dst_slot], sems.at[dst_slot]).wait()

    @pl.when(i == 0)
    def _():
        start_gather(0, 0)

    @pl.when(i + 1 < n_used)
    def _():
        start_gather(i + 1, 1 - slot)

    prev = tile_expert_ref[jnp.maximum(i - 1, 0)]
    new_expert = (i == 0) | (tile_expert_ref[i] != prev)

    @pl.when(new_expert)
    def _():
        wg_b[...] = wg_ref[0].astype(bf16)
        wu_b[...] = wu_ref[0].astype(bf16)
        wd_b[...] = wd_ref[0].astype(bf16)

    @pl.when(i < n_used)
    def _():
        wait_gather(slot)
        x = xbuf[slot].astype(bf16)
        gate = _dot(x, wg_b[...])
        up = _dot(x, wu_b[...])
        act = (gate * _sigmoid(gate) * up).astype(bf16)
        ys_ref[...] = _dot(act, wd_b[...])

    @pl.when(i >= n_used)
    def _():
        ys_ref[...] = jnp.zeros_like(ys_ref)


def _expert_call(tile_expert, n_used, row_token, h1, wg, wu, wd):
    n_tiles = tile_expert.shape[0]
    w_map = lambda i, te, nu, rt: (te[i], 0, 0)
    grid_spec = pltpu.PrefetchScalarGridSpec(
        num_scalar_prefetch=3,
        grid=(n_tiles,),
        in_specs=[pl.BlockSpec(memory_space=pl.ANY),
                  pl.BlockSpec((1, D_MODEL, EXPERT_FF), w_map),
                  pl.BlockSpec((1, D_MODEL, EXPERT_FF), w_map),
                  pl.BlockSpec((1, EXPERT_FF, D_MODEL), w_map)],
        out_specs=pl.BlockSpec((EXPERT_TILE, D_MODEL), lambda i, te, nu, rt: (i, 0)),
        scratch_shapes=[pltpu.VMEM((2, EXPERT_TILE, D_MODEL), jnp.float32),
                        pltpu.SemaphoreType.DMA((2,)),
                        pltpu.VMEM((D_MODEL, EXPERT_FF), jnp.bfloat16),
                        pltpu.VMEM((D_MODEL, EXPERT_FF), jnp.bfloat16),
                        pltpu.VMEM((EXPERT_FF, D_MODEL), jnp.bfloat16)],
    )
    return pl.pallas_call(
        _expert_kernel,
        grid_spec=grid_spec,
        out_shape=jax.ShapeDtypeStruct((n_tiles * EXPERT_TILE, D_MODEL), jnp.float32),
        compiler_params=pltpu.CompilerParams(dimension_semantics=("arbitrary",),
                                             vmem_limit_bytes=VMEM_LIMIT),
        name="experts",
    )(tile_expert, n_used, row_token, h1, wg, wu, wd)


def _combine_kernel(pos_ref, h1_ref, route_ref, pp_ref, ps_ref, ys_hbm, w_ple_proj_ref, w_ple_gate_ref,
                    ln2_g_ref, ln2_b_ref, yp_ref, ys_out_ref, ybuf, sems, *, n_prompt_tiles):
    i = pl.program_id(0)
    n_steps = pl.num_programs(0)
    bf16 = jnp.bfloat16
    slot = lax.rem(i, 2)

    def start_gather(tile, dst_slot):
        def issue(r, carry):
            pair = 2 * (tile * TOKEN_TILE + r)
            for which in range(2):
                pltpu.make_async_copy(ys_hbm.at[pl.ds(pos_ref[pair + which], 1)],
                                      ybuf.at[dst_slot, which, pl.ds(r, 1)], sems.at[dst_slot]).start()
            return carry
        lax.fori_loop(0, TOKEN_TILE, issue, 0, unroll=4)

    def wait_gather(dst_slot):
        for which in range(2):
            pltpu.make_async_copy(ys_hbm.at[pl.ds(0, TOKEN_TILE)], ybuf.at[dst_slot, which],
                                  sems.at[dst_slot]).wait()

    @pl.when(i == 0)
    def _():
        start_gather(0, 0)

    @pl.when(i + 1 < n_steps)
    def _():
        start_gather(i + 1, 1 - slot)

    h1 = h1_ref[...]
    is_prompt = i < n_prompt_tiles
    p = jnp.where(is_prompt, pp_ref[...], ps_ref[...])
    pe = _dot(p.astype(bf16), w_ple_proj_ref[...]) * _sigmoid(_dot(h1.astype(bf16), w_ple_gate_ref[...]))
    route = route_ref[...]
    w1 = route[:, 2:3]
    w2 = route[:, 3:4]
    wait_gather(slot)
    total = DN_ALPHA * h1 + pe + w1 * ybuf[slot, 0] + w2 * ybuf[slot, 1]
    out = _layer_norm(total, ln2_g_ref[...], ln2_b_ref[...])

    @pl.when(is_prompt)
    def _():
        yp_ref[...] = out

    @pl.when(jnp.logical_not(is_prompt))
    def _():
        ys_out_ref[...] = out


def _combine_call(pos, h1, route, p_prompt, p_sample, ys, w_ple_proj, w_ple_gate, ln2_g, ln2_b):
    n_tok = h1.shape[0]
    n_prompt_tiles = p_prompt.shape[0] // TOKEN_TILE
    n_sample_tiles = p_sample.shape[0] // TOKEN_TILE
    n_steps = n_tok // TOKEN_TILE
    assert n_steps == n_prompt_tiles + n_sample_tiles
    tok_spec = lambda cols: pl.BlockSpec((TOKEN_TILE, cols), lambda i, pos: (i, 0))
    prompt_spec = lambda cols: pl.BlockSpec((TOKEN_TILE, cols),
                                            lambda i, pos: (jnp.minimum(i, n_prompt_tiles - 1), 0))
    sample_spec = lambda cols: pl.BlockSpec((TOKEN_TILE, cols),
                                            lambda i, pos: (jnp.maximum(i - n_prompt_tiles, 0), 0))
    const_spec = lambda shape: pl.BlockSpec(shape, lambda i, pos: (0, 0))
    grid_spec = pltpu.PrefetchScalarGridSpec(
        num_scalar_prefetch=1,
        grid=(n_steps,),
        in_specs=[tok_spec(D_MODEL), tok_spec(LANES), prompt_spec(PLE_DIM), sample_spec(PLE_DIM),
                  pl.BlockSpec(memory_space=pl.ANY), const_spec(w_ple_proj.shape), const_spec(w_ple_gate.shape),
                  const_spec(ln2_g.shape), const_spec(ln2_b.shape)],
        out_specs=(prompt_spec(D_MODEL), sample_spec(D_MODEL)),
        scratch_shapes=[pltpu.VMEM((2, 2, TOKEN_TILE, D_MODEL), jnp.float32), pltpu.SemaphoreType.DMA((2,))],
    )
    return pl.pallas_call(
        functools.partial(_combine_kernel, n_prompt_tiles=n_prompt_tiles),
        grid_spec=grid_spec,
        out_shape=(jax.ShapeDtypeStruct((n_prompt_tiles * TOKEN_TILE, D_MODEL), jnp.float32),
                   jax.ShapeDtypeStruct((n_sample_tiles * TOKEN_TILE, D_MODEL), jnp.float32)),
        compiler_params=pltpu.CompilerParams(dimension_semantics=("arbitrary",),
                                             vmem_limit_bytes=VMEM_LIMIT),
        name="combine",
    )(pos, h1, route, p_prompt, p_sample, ys, w_ple_proj, w_ple_gate, ln2_g, ln2_b)


def _dispatch_plan(route, n_tiles):
    experts = route[:, :2].astype(jnp.int32).reshape(-1)
    n_pairs = experts.shape[0]
    onehot = (experts[:, None] == jnp.arange(N_EXPERTS, dtype=jnp.int32)[None, :]).astype(jnp.int32)
    running = jnp.cumsum(onehot, axis=0)
    rank = jnp.sum(running * onehot, axis=1) - 1
    counts = running[-1]
    tiles_per = (counts + EXPERT_TILE - 1) // EXPERT_TILE
    tile_end = jnp.cumsum(tiles_per)
    row_start = (tile_end - tiles_per) * EXPERT_TILE
    pos = row_start[experts] + rank
    n_used = tile_end[-1]
    tile_ids = jnp.minimum(jnp.arange(n_tiles, dtype=jnp.int32), n_used - 1)
    tile_expert = jnp.sum((tile_ids[:, None] >= tile_end[None, :]).astype(jnp.int32), axis=1)

    order = jnp.argsort(experts, stable=True).astype(jnp.int32)
    pair_start = jnp.cumsum(counts) - counts
    rows = jnp.arange(n_tiles * EXPERT_TILE, dtype=jnp.int32)
    row_expert = jnp.repeat(tile_expert, EXPERT_TILE)
    k = rows - row_start[row_expert]
    valid = k < counts[row_expert]
    src_pair = order[jnp.clip(pair_start[row_expert] + k, 0, n_pairs - 1)]
    row_token = jnp.where(valid, src_pair // 2, 0)
    return (pos.astype(jnp.int32), row_token.astype(jnp.int32), tile_expert.astype(jnp.int32),
            n_used.reshape(1).astype(jnp.int32))


def kernel(x_prompt, x_sample, p_prompt, p_sample, state_ret, state_conv, ln_in_g, ln_in_b, w_in, w_ret_out,
           conv_w, conv_b, conv_ln_g, conv_ln_b, w_conv_out, w_out, ln1_g, ln1_b, w_route_g, b_route_g,
           w_route_e, b_route_e, w_exp_gate, w_exp_up, w_exp_down, w_ple_proj, w_ple_gate, ln2_g, ln2_b):
    assert w_in.shape[0] == DEPTH == 1
    bf16 = jnp.bfloat16
    batch, seq, _ = x_prompt.shape
    dec_batch, dec_seq, _ = x_sample.shape
    assert dec_seq == CHUNK and seq % TOKEN_TILE == 0 and TOKEN_TILE % CHUNK == 0
    row = lambda a: a.reshape(1, -1)
    n_prompt = batch * seq
    n_sample = dec_batch * dec_seq
    n_tok = n_prompt + n_sample

    w_route = jnp.concatenate(
        [w_route_g[0], jnp.transpose(w_route_e[0], (1, 0, 2)).reshape(D_MODEL, N_EXPERTS),
         jnp.zeros((D_MODEL, LANES - ROUTE_COLS), jnp.float32)], axis=1)
    w_route_hi = w_route.astype(bf16)
    w_route_lo = (w_route - w_route_hi.astype(jnp.float32)).astype(bf16)
    b_route = jnp.concatenate([b_route_g[0], b_route_e[0].reshape(-1),
                               jnp.zeros((LANES - ROUTE_COLS,), jnp.float32)]).reshape(1, LANES)
    conv_w_pad = jnp.concatenate([conv_w[0], jnp.zeros((HIST_ROWS - CONV_WIDTH, D_MODEL), jnp.float32)], axis=0)
    weights = (row(ln_in_g), row(ln_in_b), w_in[0].astype(bf16), w_ret_out[0].astype(bf16), conv_w_pad,
               row(conv_b[0]), row(conv_ln_g[0]), row(conv_ln_b[0]), w_conv_out[0].astype(bf16),
               w_out[0].astype(bf16), row(ln1_g[0]), row(ln1_b[0]),
               jnp.concatenate([w_route_hi, w_route_lo], axis=1), w_route_hi, b_route)

    h1 = jnp.zeros((n_tok, D_MODEL), jnp.float32)
    route = jnp.zeros((n_tok, LANES), jnp.float32)

    tables_p = _mix_tables(jnp.arange(seq, dtype=jnp.int32), 1, TOKEN_TILE)
    rinit_p = jnp.zeros((batch, RET_HEADS, RET_DK, RET_DV), jnp.float32)
    cinit_p = jnp.zeros((batch, HIST_ROWS, D_MODEL), jnp.float32)
    h1, route, rnew_p, cnew_p = _mix_call(
        x_prompt.reshape(n_prompt, D_MODEL), rinit_p, cinit_p, tables_p, weights, h1, route,
        n_seq=1, seq_rows=TOKEN_TILE, n_groups=batch, n_steps=seq // TOKEN_TILE, table_per_step=True,
        tile_offset=0)

    seqs_per_tile = TOKEN_TILE // dec_seq
    assert dec_batch % seqs_per_tile == 0
    pos_s = jnp.tile(PAST_LEN + jnp.arange(dec_seq, dtype=jnp.int32), seqs_per_tile)
    tables_s = _mix_tables(pos_s, seqs_per_tile, dec_seq)
    cinit_s = jnp.pad(state_conv[0], ((0, 0), (HIST_PAD, 0), (0, 0)))
    h1, route, rnew_s, cnew_s = _mix_call(
        x_sample.reshape(n_sample, D_MODEL), state_ret[0], cinit_s, tables_s, weights, h1, route,
        n_seq=seqs_per_tile, seq_rows=dec_seq, n_groups=dec_batch // seqs_per_tile, n_steps=1,
        table_per_step=False, tile_offset=n_prompt // TOKEN_TILE)

    n_tiles = (2 * n_tok) // EXPERT_TILE + N_EXPERTS
    pos, row_token, tile_expert, n_used = _dispatch_plan(route, n_tiles)
    ys = _expert_call(tile_expert, n_used, row_token, h1,
                      w_exp_gate[0].reshape(N_EXPERTS, D_MODEL, EXPERT_FF),
                      w_exp_up[0].reshape(N_EXPERTS, D_MODEL, EXPERT_FF),
                      w_exp_down[0].reshape(N_EXPERTS, EXPERT_FF, D_MODEL))
    y_p, y_s = _combine_call(pos, h1, route, p_prompt[0].reshape(n_prompt, PLE_DIM),
                             p_sample[0].reshape(n_sample, PLE_DIM), ys, w_ple_proj[0].astype(bf16),
                             w_ple_gate[0].astype(bf16), row(ln2_g[0]), row(ln2_b[0]))

    return (y_p.reshape(batch, seq, D_MODEL), y_s.reshape(dec_batch, dec_seq, D_MODEL), rnew_p[None],
            cnew_p[None, :, HIST_PAD:, :], rnew_s[None], cnew_s[None, :, HIST_PAD:, :])
```

```python
import functools
import math

import jax
import jax.numpy as jnp
from jax import lax
from jax.experimental import pallas as pl
from jax.experimental.pallas import tpu as pltpu

D_MODEL = 1024
RET_HEADS = 4
RET_DK = 128
RET_DV = 256
RET_QK = RET_HEADS * RET_DK
RET_V = RET_HEADS * RET_DV
CHUNK = 64
CONV_WIDTH = 31
CONV_HIST = CONV_WIDTH - 1
HIST_ROWS = 32
HIST_PAD = HIST_ROWS - CONV_HIST
N_GROUPS = 4
EXPERTS_PER_GROUP = 8
N_EXPERTS = N_GROUPS * EXPERTS_PER_GROUP
EXPERT_FF = 512
PLE_DIM = 256
PAST_LEN = 4096
LN_EPS = 1e-5
ROPE_BASE = 10000.0
DEPTH = 1
DN_ALPHA = float((2 * DEPTH) ** 0.25)

LANES = 128
SUBLANES = 8
TOKEN_TILE = 256
EXPERT_TILE = 256
ROW_TILE_ROWS = D_MODEL // LANES
CONV_ROW_BLOCK = 64
VMEM_LIMIT = 56 * 1024 * 1024

C_Q = 0
C_K = RET_QK
C_V = 2 * RET_QK
C_G = C_V + RET_V
C_GLU_A = C_G + RET_V
C_GLU_B = C_GLU_A + D_MODEL
C_GA = C_GLU_B + D_MODEL
C_GB = C_GA + D_MODEL

ROUTE_COLS = N_GROUPS + N_EXPERTS
NEG_BIG = -1e30


def _sigmoid(x):
    return 1.0 / (1.0 + jnp.exp(-x))


def _layer_norm(x, g, b):
    mu = jnp.mean(x, axis=-1, keepdims=True)
    xc = x - mu
    var = jnp.mean(xc * xc, axis=-1, keepdims=True)
    return xc * lax.rsqrt(var + LN_EPS) * g + b


def _dot(a, b):
    return jnp.dot(a, b, preferred_element_type=jnp.float32)


def _store_row_tiles(ref, x):
    n = x.shape[0]
    for j in range(ROW_TILE_ROWS):
        ref[pl.ds(j, n, stride=ROW_TILE_ROWS), :] = x[:, j * LANES:(j + 1) * LANES]


def _load_row_tiles(ref, n):
    return jnp.concatenate([ref[pl.ds(j, n, stride=ROW_TILE_ROWS), :] for j in range(ROW_TILE_ROWS)], axis=1)


def _mix_kernel(x_ref, rinit_ref, cinit_ref, cq_ref, sq_ref, ck_ref, sk_ref,
                xi_ref, zeta_ref, mask_ref,
                ln_in_g_ref, ln_in_b_ref, w_in_ref, w_ret_out_ref, conv_w_ref, conv_b_ref,
                conv_ln_g_ref, conv_ln_b_ref, w_conv_out_ref, w_out_ref, ln1_g_ref, ln1_b_ref,
                w_route_ref, w_route_hi_ref, b_route_ref, h1_alias_ref, route_alias_ref,
                h1_ref, route_ref, rnew_ref, cnew_ref,
                r_scr, ubuf, c_scr,
                *, n_seq, seq_rows, state_decay):
    del h1_alias_ref, route_alias_ref
    t = pl.program_id(1)
    last_t = pl.num_programs(1) - 1
    bf16 = jnp.bfloat16

    @pl.when(t == 0)
    def _():
        r_scr[...] = rinit_ref[...]
        ubuf[:, 0:HIST_ROWS, :] = cinit_ref[...]

    h = _layer_norm(x_ref[...], ln_in_g_ref[...], ln_in_b_ref[...])
    hb = h.astype(bf16)

    q = _dot(hb, w_in_ref[:, C_Q:C_Q + RET_QK])
    k = _dot(hb, w_in_ref[:, C_K:C_K + RET_QK])
    v = _dot(hb, w_in_ref[:, C_V:C_V + RET_V])
    g = _dot(hb, w_in_ref[:, C_G:C_G + RET_V])
    g = g * _sigmoid(g)
    cq, sq, ck, sk = cq_ref[...], sq_ref[...], ck_ref[...], sk_ref[...]

    y_ret = None
    for hd in range(RET_HEADS):
        qh = q[:, hd * RET_DK:(hd + 1) * RET_DK]
        kh = k[:, hd * RET_DK:(hd + 1) * RET_DK]
        qr = qh * cq + pltpu.roll(qh, RET_DK // 2, 1) * sq
        kr = kh * ck + pltpu.roll(kh, RET_DK // 2, 1) * sk
        vb = v[:, hd * RET_DV:(hd + 1) * RET_DV].astype(bf16)
        s = lax.dot_general(qr.astype(bf16), kr.astype(bf16), (((1,), (1,)), ((), ())),
                            preferred_element_type=jnp.float32)
        inner = _dot((s * mask_ref[hd]).astype(bf16), vb)
        qx = (qr * xi_ref[hd]).astype(bf16)
        kz = (kr * zeta_ref[hd]).astype(bf16)
        o_parts = []
        for sq_i in range(n_seq):
            rows = slice(sq_i * seq_rows, (sq_i + 1) * seq_rows)
            r_old = r_scr[sq_i, hd]
            o_parts.append(inner[rows] + _dot(qx[rows], r_old.astype(bf16)))
            kv = lax.dot_general(kz[rows], vb[rows], (((0,), (0,)), ((), ())),
                                 preferred_element_type=jnp.float32)
            r_scr[sq_i, hd] = state_decay[hd] * r_old + kv
        o = o_parts[0] if n_seq == 1 else jnp.concatenate(o_parts, axis=0)
        on = _layer_norm(o, 1.0, 0.0)
        gated = (on * g[:, hd * RET_DV:(hd + 1) * RET_DV]).astype(bf16)
        part = _dot(gated, w_ret_out_ref[hd * RET_DV:(hd + 1) * RET_DV, :])
        y_ret = part if y_ret is None else y_ret + part

    u = _dot(hb, w_in_ref[:, C_GLU_A:C_GLU_A + D_MODEL]) * _sigmoid(
        _dot(hb, w_in_ref[:, C_GLU_B:C_GLU_B + D_MODEL]))
    for sq_i in range(n_seq):
        ubuf[sq_i, HIST_ROWS:HIST_ROWS + seq_rows, :] = u[sq_i * seq_rows:(sq_i + 1) * seq_rows]

    row_block = min(CONV_ROW_BLOCK, seq_rows)

    def conv_lane_chunk(j, carry):
        lane = pl.ds(pl.multiple_of(j * LANES, LANES), LANES)
        for sq_i in range(n_seq):
            for rb in range(seq_rows // row_block):
                r0 = rb * row_block
                acc = jnp.zeros((row_block, LANES), jnp.float32)
                for sub in range(SUBLANES):
                    taps = [(a, SUBLANES * a + sub - HIST_PAD) for a in range(HIST_ROWS // SUBLANES + 1)]
                    taps = [(a, kk) for a, kk in taps if 0 <= kk < CONV_WIDTH]
                    span = row_block + SUBLANES * max(a for a, _ in taps)
                    shifted = ubuf[sq_i, r0 + sub:r0 + sub + span, lane]
                    for a, kk in taps:
                        acc = acc + conv_w_ref[kk:kk + 1, lane] * shifted[SUBLANES * a:SUBLANES * a + row_block]
                c_scr[sq_i * seq_rows + r0:sq_i * seq_rows + r0 + row_block, lane] = acc
        return carry

    lax.fori_loop(0, D_MODEL // LANES, conv_lane_chunk, 0)

    for sq_i in range(n_seq):
        ubuf[sq_i, 0:HIST_ROWS, :] = ubuf[sq_i, seq_rows:seq_rows + HIST_ROWS, :]

    @pl.when(t == last_t)
    def _():
        rnew_ref[...] = r_scr[...]
        cnew_ref[...] = ubuf[:, 0:HIST_ROWS, :]

    c = _layer_norm(c_scr[...] + conv_b_ref[...], conv_ln_g_ref[...], conv_ln_b_ref[...])
    c = c * _sigmoid(c)
    y_conv = _dot(c.astype(bf16), w_conv_out_ref[...])

    ga = _sigmoid(_dot(hb, w_in_ref[:, C_GA:C_GA + D_MODEL]))
    gb = _sigmoid(_dot(hb, w_in_ref[:, C_GB:C_GB + D_MODEL]))
    merged = (ga * y_ret + gb * y_conv).astype(bf16)
    h1 = _layer_norm(DN_ALPHA * h + _dot(merged, w_out_ref[...]), ln1_g_ref[...], ln1_b_ref[...])
    _store_row_tiles(h1_ref, h1)

    h1_hi = h1.astype(bf16)
    h1_lo = (h1 - h1_hi.astype(jnp.float32)).astype(bf16)
    both = _dot(h1_hi, w_route_ref[...])
    logits = both[:, :LANES] + both[:, LANES:] + _dot(h1_lo, w_route_hi_ref[...]) + b_route_ref[...]
    lane = lax.broadcasted_iota(jnp.int32, logits.shape, 1)
    lane_f = lane.astype(jnp.float32)
    is_group = lane < N_GROUPS
    gl = jnp.where(is_group, logits, NEG_BIG)
    gmax = jnp.max(gl, axis=-1, keepdims=True)
    gexp = jnp.where(is_group, jnp.exp(gl - gmax), 0.0)
    gprob = gexp / jnp.sum(gexp, axis=-1, keepdims=True)
    gp = jnp.max(gprob, axis=-1, keepdims=True)
    gi = jnp.min(jnp.where(is_group & (gprob == gp), lane_f, float(LANES)), axis=-1, keepdims=True)
    e_lo = N_GROUPS + EXPERTS_PER_GROUP * gi
    in_group = (lane_f >= e_lo) & (lane_f < e_lo + EXPERTS_PER_GROUP)
    el = jnp.where(in_group, logits, NEG_BIG)
    ev1 = jnp.max(el, axis=-1, keepdims=True)
    ei1 = jnp.min(jnp.where(in_group & (el == ev1), lane_f, float(LANES)), axis=-1, keepdims=True)
    rest = in_group & (lane_f != ei1)
    el2 = jnp.where(rest, logits, NEG_BIG)
    ev2 = jnp.max(el2, axis=-1, keepdims=True)
    ei2 = jnp.min(jnp.where(rest & (el2 == ev2), lane_f, float(LANES)), axis=-1, keepdims=True)
    x2 = jnp.exp(ev2 - ev1)
    w1 = gp / (1.0 + x2)
    w2 = w1 * x2
    route = jnp.where(lane == 0, ei1 - N_GROUPS, 0.0)
    route = jnp.where(lane == 1, ei2 - N_GROUPS, route)
    route = jnp.where(lane == 2, w1, route)
    route = jnp.where(lane == 3, w2, route)
    route_ref[...] = route


def _const_spec(shape):
    zeros = (0,) * len(shape)
    return pl.BlockSpec(shape, lambda b, t: zeros, pipeline_mode=pl.Buffered(1))


def _mix_call(x, rinit, cinit, tables, weights, h1_buf, route_buf, *, n_seq, seq_rows, n_groups, n_steps,
              table_per_step, tile_offset):
    tt = n_seq * seq_rows
    assert tt == TOKEN_TILE and x.shape[0] == n_groups * n_steps * tt
    log_decay = [math.log1p(-(2.0 ** (-5.0 - hd))) for hd in range(RET_HEADS)]
    state_decay = tuple(math.exp(lg * seq_rows) for lg in log_decay)
    cq, sq, ck, sk, xi, zeta, mask = tables

    in_tok_spec = pl.BlockSpec((tt, D_MODEL), lambda b, t: (b * n_steps + t, 0))
    out_tok_spec = lambda rows: pl.BlockSpec((rows, LANES), lambda b, t: (tile_offset + b * n_steps + t, 0))
    rope_spec = pl.BlockSpec((tt, LANES), (lambda b, t: (t, 0)) if table_per_step else (lambda b, t: (0, 0)))
    state_r_spec = pl.BlockSpec((n_seq, RET_HEADS, RET_DK, RET_DV), lambda b, t: (b, 0, 0, 0))
    state_c_spec = pl.BlockSpec((n_seq, HIST_ROWS, D_MODEL), lambda b, t: (b, 0, 0))
    any_spec = pl.BlockSpec(memory_space=pl.ANY)

    in_specs = [in_tok_spec, state_r_spec, state_c_spec, rope_spec, rope_spec, rope_spec, rope_spec,
                _const_spec(xi.shape), _const_spec(zeta.shape), _const_spec(mask.shape)]
    in_specs += [_const_spec(w.shape) for w in weights]
    in_specs += [any_spec, any_spec]
    n_in = len(in_specs)
    out_shape = (jax.ShapeDtypeStruct(h1_buf.shape, jnp.float32),
                 jax.ShapeDtypeStruct(route_buf.shape, jnp.float32),
                 jax.ShapeDtypeStruct(rinit.shape, jnp.float32),
                 jax.ShapeDtypeStruct(cinit.shape, jnp.float32))
    out_specs = (out_tok_spec(tt * ROW_TILE_ROWS), out_tok_spec(tt), state_r_spec, state_c_spec)
    body = functools.partial(_mix_kernel, n_seq=n_seq, seq_rows=seq_rows, state_decay=state_decay)
    return pl.pallas_call(
        body,
        grid=(n_groups, n_steps),
        in_specs=in_specs,
        out_specs=out_specs,
        out_shape=out_shape,
        input_output_aliases={n_in - 2: 0, n_in - 1: 1},
        scratch_shapes=[pltpu.VMEM((n_seq, RET_HEADS, RET_DK, RET_DV), jnp.float32),
                        pltpu.VMEM((n_seq, HIST_ROWS + seq_rows, D_MODEL), jnp.float32),
                        pltpu.VMEM((tt, D_MODEL), jnp.float32)],
        compiler_params=pltpu.CompilerParams(dimension_semantics=("arbitrary", "arbitrary"),
                                             vmem_limit_bytes=VMEM_LIMIT),
        name="mix",
    )(x, rinit, cinit, cq, sq, ck, sk, xi, zeta, mask, *weights, h1_buf, route_buf)


def _mix_tables(positions, n_seq, seq_rows):
    half = RET_DK // 2
    inv = ROPE_BASE ** (-jnp.arange(half, dtype=jnp.float32) / half)
    ang = positions.astype(jnp.float32)[:, None] * inv[None, :]
    cos, sin = jnp.cos(ang), jnp.sin(ang)
    cq = jnp.concatenate([cos, cos], axis=1)
    sq = jnp.concatenate([-sin, sin], axis=1)
    scale = RET_DK ** -0.5
    lg = jnp.log1p(-jnp.exp2(-5.0 - jnp.arange(RET_HEADS, dtype=jnp.float32)))
    tt = n_seq * seq_rows
    i = jnp.arange(tt)
    loc = (i % seq_rows).astype(jnp.float32)
    xi = jnp.exp(lg[:, None] * (loc + 1.0)[None, :])
    zeta = jnp.exp(lg[:, None] * (seq_rows - 1.0 - loc)[None, :])
    xi = jnp.broadcast_to(xi[:, :, None], (RET_HEADS, tt, LANES))
    zeta = jnp.broadcast_to(zeta[:, :, None], (RET_HEADS, tt, LANES))
    same_seq = (i[:, None] // seq_rows) == (i[None, :] // seq_rows)
    visible = same_seq & ((i[None, :] // CHUNK) <= (i[:, None] // CHUNK))
    dist = jnp.abs(i[:, None] - i[None, :]).astype(jnp.float32)
    mask = jnp.where(visible[None], jnp.exp(lg[:, None, None] * dist[None]), 0.0)
    return cq, sq, cq * scale, sq * scale, xi, zeta, mask


def _expert_kernel(tile_expert_ref, n_used_ref, row_token_ref, h1_hbm, wg_ref, wu_ref, wd_ref, ys_ref,
                   xbuf, sems, wg_b, wu_b, wd_b):
    i = pl.program_id(0)
    n_used = n_used_ref[0]
    bf16 = jnp.bfloat16
    slot = lax.rem(i, 2)

    def start_gather(tile, dst_slot):
        def issue(r, carry):
            src = pl.multiple_of(row_token_ref[tile * EXPERT_TILE + r] * ROW_TILE_ROWS, ROW_TILE_ROWS)
            dst = pl.multiple_of(r * ROW_TILE_ROWS, ROW_TILE_ROWS)
            pltpu.make_async_copy(h1_hbm.at[pl.ds(src, ROW_TILE_ROWS)],
                                  xbuf.at[dst_slot, pl.ds(dst, ROW_TILE_ROWS)], sems.at[dst_slot]).start()
            return carry
        lax.fori_loop(0, EXPERT_TILE, issue, 0, unroll=8)

    def wait_gather(dst_slot):
        pltpu.make_async_copy(h1_hbm.at[pl.ds(0, EXPERT_TILE * ROW_TILE_ROWS)], xbuf.at[dst_slot],
                              sems.at[dst_slot]).wait()

    @pl.when(i == 0)
    def _():
        start_gather(0, 0)

    @pl.when(i + 1 < n_used)
    def _():
        start_gather(i + 1, 1 - slot)

    prev = tile_expert_ref[jnp.maximum(i - 1, 0)]
    new_expert = (i == 0) | (tile_expert_ref[i] != prev)

    @pl.when(new_expert)
    def _():
        wg_b[...] = wg_ref[0].astype(bf16)
        wu_b[...] = wu_ref[0].astype(bf16)
        wd_b[...] = wd_ref[0].astype(bf16)

    @pl.when(i < n_used)
    def _():
        wait_gather(slot)
        x = _load_row_tiles(xbuf.at[slot], EXPERT_TILE).astype(bf16)
        gate = _dot(x, wg_b[...])
        up = _dot(x, wu_b[...])
        act = (gate * _sigmoid(gate) * up).astype(bf16)
        _store_row_tiles(ys_ref, _dot(act, wd_b[...]))

    @pl.when(i >= n_used)
    def _():
        ys_ref[...] = jnp.zeros_like(ys_ref)


def _expert_call(tile_expert, n_used, row_token, h1, wg, wu, wd):
    n_tiles = tile_expert.shape[0]
    w_map = lambda i, te, nu, rt: (te[i], 0, 0)
    grid_spec = pltpu.PrefetchScalarGridSpec(
        num_scalar_prefetch=3,
        grid=(n_tiles,),
        in_specs=[pl.BlockSpec(memory_space=pl.ANY),
                  pl.BlockSpec((1, D_MODEL, EXPERT_FF), w_map),
                  pl.BlockSpec((1, D_MODEL, EXPERT_FF), w_map),
                  pl.BlockSpec((1, EXPERT_FF, D_MODEL), w_map)],
        out_specs=pl.BlockSpec((EXPERT_TILE * ROW_TILE_ROWS, LANES), lambda i, te, nu, rt: (i, 0)),
        scratch_shapes=[pltpu.VMEM((2, EXPERT_TILE * ROW_TILE_ROWS, LANES), jnp.float32),
                        pltpu.SemaphoreType.DMA((2,)),
                        pltpu.VMEM((D_MODEL, EXPERT_FF), jnp.bfloat16),
                        pltpu.VMEM((D_MODEL, EXPERT_FF), jnp.bfloat16),
                        pltpu.VMEM((EXPERT_FF, D_MODEL), jnp.bfloat16)],
    )
    return pl.pallas_call(
        _expert_kernel,
        grid_spec=grid_spec,
        out_shape=jax.ShapeDtypeStruct((n_tiles * EXPERT_TILE * ROW_TILE_ROWS, LANES), jnp.float32),
        compiler_params=pltpu.CompilerParams(dimension_semantics=("arbitrary",),
                                             vmem_limit_bytes=VMEM_LIMIT),
        name="experts",
    )(tile_expert, n_used, row_token, h1, wg, wu, wd)


def _combine_kernel(pos_ref, h1_ref, route_ref, pp_ref, ps_ref, ys_hbm, w_ple_proj_ref, w_ple_gate_ref,
                    ln2_g_ref, ln2_b_ref, yp_ref, ys_out_ref, ybuf, sems, *, n_prompt_tiles):
    i = pl.program_id(0)
    n_steps = pl.num_programs(0)
    bf16 = jnp.bfloat16
    slot = lax.rem(i, 2)

    def start_gather(tile, dst_slot):
        def issue(r, carry):
            pair = 2 * (tile * TOKEN_TILE + r)
            dst = pl.multiple_of(r * ROW_TILE_ROWS, ROW_TILE_ROWS)
            for which in range(2):
                src = pl.multiple_of(pos_ref[pair + which] * ROW_TILE_ROWS, ROW_TILE_ROWS)
                pltpu.make_async_copy(ys_hbm.at[pl.ds(src, ROW_TILE_ROWS)],
                                      ybuf.at[dst_slot, which, pl.ds(dst, ROW_TILE_ROWS)],
                                      sems.at[dst_slot]).start()
            return carry
        lax.fori_loop(0, TOKEN_TILE, issue, 0, unroll=4)

    def wait_gather(dst_slot):
        for which in range(2):
            pltpu.make_async_copy(ys_hbm.at[pl.ds(0, TOKEN_TILE * ROW_TILE_ROWS)], ybuf.at[dst_slot, which],
                                  sems.at[dst_slot]).wait()

    @pl.when(i == 0)
    def _():
        start_gather(0, 0)

    @pl.when(i + 1 < n_steps)
    def _():
        start_gather(i + 1, 1 - slot)

    h1 = _load_row_tiles(h1_ref, TOKEN_TILE)
    is_prompt = i < n_prompt_tiles
    p = jnp.where(is_prompt, pp_ref[...], ps_ref[...])
    pe = _dot(p.astype(bf16), w_ple_proj_ref[...]) * _sigmoid(_dot(h1.astype(bf16), w_ple_gate_ref[...]))
    route = route_ref[...]
    w1 = route[:, 2:3]
    w2 = route[:, 3:4]
    wait_gather(slot)
    y1 = _load_row_tiles(ybuf.at[slot, 0], TOKEN_TILE)
    y2 = _load_row_tiles(ybuf.at[slot, 1], TOKEN_TILE)
    total = DN_ALPHA * h1 + pe + w1 * y1 + w2 * y2
    out = _layer_norm(total, ln2_g_ref[...], ln2_b_ref[...])

    @pl.when(is_prompt)
    def _():
        yp_ref[...] = out

    @pl.when(jnp.logical_not(is_prompt))
    def _():
        ys_out_ref[...] = out


def _combine_call(pos, h1, route, p_prompt, p_sample, ys, w_ple_proj, w_ple_gate, ln2_g, ln2_b):
    n_tok = route.shape[0]
    n_prompt_tiles = p_prompt.shape[0] // TOKEN_TILE
    n_sample_tiles = p_sample.shape[0] // TOKEN_TILE
    n_steps = n_tok // TOKEN_TILE
    assert n_steps == n_prompt_tiles + n_sample_tiles
    tok_spec = lambda cols: pl.BlockSpec((TOKEN_TILE, cols), lambda i, pos: (i, 0))
    prompt_spec = lambda cols: pl.BlockSpec((TOKEN_TILE, cols),
                                            lambda i, pos: (jnp.minimum(i, n_prompt_tiles - 1), 0))
    sample_spec = lambda cols: pl.BlockSpec((TOKEN_TILE, cols),
                                            lambda i, pos: (jnp.maximum(i - n_prompt_tiles, 0), 0))
    const_spec = lambda shape: pl.BlockSpec(shape, lambda i, pos: (0, 0))
    grid_spec = pltpu.PrefetchScalarGridSpec(
        num_scalar_prefetch=1,
        grid=(n_steps,),
        in_specs=[pl.BlockSpec((TOKEN_TILE * ROW_TILE_ROWS, LANES), lambda i, pos: (i, 0)), tok_spec(LANES),
                  prompt_spec(PLE_DIM), sample_spec(PLE_DIM),
                  pl.BlockSpec(memory_space=pl.ANY), const_spec(w_ple_proj.shape), const_spec(w_ple_gate.shape),
                  const_spec(ln2_g.shape), const_spec(ln2_b.shape)],
        out_specs=(prompt_spec(D_MODEL), sample_spec(D_MODEL)),
        scratch_shapes=[pltpu.VMEM((2, 2, TOKEN_TILE * ROW_TILE_ROWS, LANES), jnp.float32),
                        pltpu.SemaphoreType.DMA((2,))],
    )
    return pl.pallas_call(
        functools.partial(_combine_kernel, n_prompt_tiles=n_prompt_tiles),
        grid_spec=grid_spec,
        out_shape=(jax.ShapeDtypeStruct((n_prompt_tiles * TOKEN_TILE, D_MODEL), jnp.float32),
                   jax.ShapeDtypeStruct((n_sample_tiles * TOKEN_TILE, D_MODEL), jnp.float32)),
        compiler_params=pltpu.CompilerParams(dimension_semantics=("arbitrary",),
                                             vmem_limit_bytes=VMEM_LIMIT),
        name="combine",
    )(pos, h1, route, p_prompt, p_sample, ys, w_ple_proj, w_ple_gate, ln2_g, ln2_b)


def _dispatch_plan(route, n_tiles):
    experts = route[:, :2].astype(jnp.int32).reshape(-1)
    n_pairs = experts.shape[0]
    onehot = (experts[:, None] == jnp.arange(N_EXPERTS, dtype=jnp.int32)[None, :]).astype(jnp.int32)
    running = jnp.cumsum(onehot, axis=0)
    rank = jnp.sum(running * onehot, axis=1) - 1
    counts = running[-1]
    tiles_per = (counts + EXPERT_TILE - 1) // EXPERT_TILE
    tile_end = jnp.cumsum(tiles_per)
    row_start = (tile_end - tiles_per) * EXPERT_TILE
    pos = row_start[experts] + rank
    n_used = tile_end[-1]
    tile_ids = jnp.minimum(jnp.arange(n_tiles, dtype=jnp.int32), n_used - 1)
    tile_expert = jnp.sum((tile_ids[:, None] >= tile_end[None, :]).astype(jnp.int32), axis=1)

    order = jnp.argsort(experts, stable=True).astype(jnp.int32)
    pair_start = jnp.cumsum(counts) - counts
    rows = jnp.arange(n_tiles * EXPERT_TILE, dtype=jnp.int32)
    row_expert = jnp.repeat(tile_expert, EXPERT_TILE)
    k = rows - row_start[row_expert]
    valid = k < counts[row_expert]
    src_pair = order[jnp.clip(pair_start[row_expert] + k, 0, n_pairs - 1)]
    row_token = jnp.where(valid, src_pair // 2, 0)
    return (pos.astype(jnp.int32), row_token.astype(jnp.int32), tile_expert.astype(jnp.int32),
            n_used.reshape(1).astype(jnp.int32))


def kernel(x_prompt, x_sample, p_prompt, p_sample, state_ret, state_conv, ln_in_g, ln_in_b, w_in, w_ret_out,
           conv_w, conv_b, conv_ln_g, conv_ln_b, w_conv_out, w_out, ln1_g, ln1_b, w_route_g, b_route_g,
           w_route_e, b_route_e, w_exp_gate, w_exp_up, w_exp_down, w_ple_proj, w_ple_gate, ln2_g, ln2_b):
    assert w_in.shape[0] == DEPTH == 1
    bf16 = jnp.bfloat16
    batch, seq, _ = x_prompt.shape
    dec_batch, dec_seq, _ = x_sample.shape
    assert dec_seq == CHUNK and seq % TOKEN_TILE == 0 and TOKEN_TILE % CHUNK == 0
    row = lambda a: a.reshape(1, -1)
    n_prompt = batch * seq
    n_sample = dec_batch * dec_seq
    n_tok = n_prompt + n_sample

    w_route = jnp.concatenate(
        [w_route_g[0], jnp.transpose(w_route_e[0], (1, 0, 2)).reshape(D_MODEL, N_EXPERTS),
         jnp.zeros((D_MODEL, LANES - ROUTE_COLS), jnp.float32)], axis=1)
    w_route_hi = w_route.astype(bf16)
    w_route_lo = (w_route - w_route_hi.astype(jnp.float32)).astype(bf16)
    b_route = jnp.concatenate([b_route_g[0], b_route_e[0].reshape(-1),
                               jnp.zeros((LANES - ROUTE_COLS,), jnp.float32)]).reshape(1, LANES)
    conv_w_pad = jnp.concatenate([conv_w[0], jnp.zeros((HIST_ROWS - CONV_WIDTH, D_MODEL), jnp.float32)], axis=0)
    weights = (row(ln_in_g), row(ln_in_b), w_in[0].astype(bf16), w_ret_out[0].astype(bf16), conv_w_pad,
               row(conv_b[0]), row(conv_ln_g[0]), row(conv_ln_b[0]), w_conv_out[0].astype(bf16),
               w_out[0].astype(bf16), row(ln1_g[0]), row(ln1_b[0]),
               jnp.concatenate([w_route_hi, w_route_lo], axis=1), w_route_hi, b_route)

    h1 = jnp.zeros((n_tok * ROW_TILE_ROWS, LANES), jnp.float32)
    route = jnp.zeros((n_tok, LANES), jnp.float32)

    tables_p = _mix_tables(jnp.arange(seq, dtype=jnp.int32), 1, TOKEN_TILE)
    rinit_p = jnp.zeros((batch, RET_HEADS, RET_DK, RET_DV), jnp.float32)
    cinit_p = jnp.zeros((batch, HIST_ROWS, D_MODEL), jnp.float32)
    h1, route, rnew_p, cnew_p = _mix_call(
        x_prompt.reshape(n_prompt, D_MODEL), rinit_p, cinit_p, tables_p, weights, h1, route,
        n_seq=1, seq_rows=TOKEN_TILE, n_groups=batch, n_steps=seq // TOKEN_TILE, table_per_step=True,
        tile_offset=0)

    seqs_per_tile = TOKEN_TILE // dec_seq
    assert dec_batch % seqs_per_tile == 0
    pos_s = jnp.tile(PAST_LEN + jnp.arange(dec_seq, dtype=jnp.int32), seqs_per_tile)
    tables_s = _mix_tables(pos_s, seqs_per_tile, dec_seq)
    cinit_s = jnp.pad(state_conv[0], ((0, 0), (HIST_PAD, 0), (0, 0)))
    h1, route, rnew_s, cnew_s = _mix_call(
        x_sample.reshape(n_sample, D_MODEL), state_ret[0], cinit_s, tables_s, weights, h1, route,
        n_seq=seqs_per_tile, seq_rows=dec_seq, n_groups=dec_batch // seqs_per_tile, n_steps=1,
        table_per_step=False, tile_offset=n_prompt // TOKEN_TILE)

    n_tiles = (2 * n_tok) // EXPERT_TILE + N_EXPERTS
    pos, row_token, tile_expert, n_used = _dispatch_plan(route, n_tiles)
    ys = _expert_call(tile_expert, n_used, row_token, h1,
                      w_exp_gate[0].reshape(N_EXPERTS, D_MODEL, EXPERT_FF),
                      w_exp_up[0].reshape(N_EXPERTS, D_MODEL, EXPERT_FF),
                      w_exp_down[0].reshape(N_EXPERTS, EXPERT_FF, D_MODEL))
    y_p, y_s = _combine_call(pos, h1, route, p_prompt[0].reshape(n_prompt, PLE_DIM),
                             p_sample[0].reshape(n_sample, PLE_DIM), ys, w_ple_proj[0].astype(bf16),
                             w_ple_gate[0].astype(bf16), row(ln2_g[0]), row(ln2_b[0]))

    return (y_p.reshape(batch, seq, D_MODEL), y_s.reshape(dec_batch, dec_seq, D_MODEL), rnew_p[None],
            cnew_p[None, :, HIST_PAD:, :], rnew_s[None], cnew_s[None, :, HIST_PAD:, :])
```

```python
import functools
import math

import jax
import jax.numpy as jnp
import numpy as np
from jax import lax
from jax.experimental import pallas as pl
from jax.experimental.pallas import tpu as pltpu

D_MODEL = 1024
RET_HEADS = 4
RET_DK = 128
RET_DV = 256
RET_QK = RET_HEADS * RET_DK
RET_V = RET_HEADS * RET_DV
CHUNK = 64
CONV_WIDTH = 31
CONV_HIST = CONV_WIDTH - 1
HIST_ROWS = 32
HIST_PAD = HIST_ROWS - CONV_HIST
N_GROUPS = 4
EXPERTS_PER_GROUP = 8
N_EXPERTS = N_GROUPS * EXPERTS_PER_GROUP
EXPERT_FF = 512
PLE_DIM = 256
PAST_LEN = 4096
LN_EPS = 1e-5
ROPE_BASE = 10000.0
DEPTH = 1
DN_ALPHA = float((2 * DEPTH) ** 0.25)

LANES = 128
SUBLANES = 8
TOKEN_TILE = 256
EXPERT_TILE = 256
ROW_TILE_ROWS = D_MODEL // LANES
CONV_ROW_BLOCK = 64
VMEM_LIMIT = 56 * 1024 * 1024

C_Q = 0
C_K = RET_QK
C_V = 2 * RET_QK
C_G = C_V + RET_V
C_GLU_A = C_G + RET_V
C_GLU_B = C_GLU_A + D_MODEL
C_GA = C_GLU_B + D_MODEL
C_GB = C_GA + D_MODEL

ROUTE_COLS = N_GROUPS + N_EXPERTS
NEG_BIG = -1e30
NEG_LOG2_E = -math.log2(math.e)


def _sigmoid(x):
    return 1.0 / (1.0 + jnp.exp2(x * NEG_LOG2_E))


def _layer_norm(x, g, b):
    mu = jnp.mean(x, axis=-1, keepdims=True)
    xc = x - mu
    var = jnp.mean(xc * xc, axis=-1, keepdims=True)
    return xc * lax.rsqrt(var + LN_EPS) * g + b


def _dot(a, b):
    return jnp.dot(a, b, preferred_element_type=jnp.float32)


def _store_row_tiles(ref, x):
    n = x.shape[0]
    for j in range(ROW_TILE_ROWS):
        ref[pl.ds(j, n, stride=ROW_TILE_ROWS), :] = x[:, j * LANES:(j + 1) * LANES]


def _load_row_tiles(ref, n):
    return jnp.concatenate([ref[pl.ds(j, n, stride=ROW_TILE_ROWS), :] for j in range(ROW_TILE_ROWS)], axis=1)


def _mix_kernel(x_ref, rinit_ref, cinit_ref, cq_ref, sq_ref, ck_ref, sk_ref,
                xi_ref, zeta_ref, mask_ref,
                ln_in_g_ref, ln_in_b_ref, w_in_ref, w_ret_out_ref, conv_w_ref, conv_b_ref,
                conv_ln_g_ref, conv_ln_b_ref, w_conv_out_ref, w_out_ref, ln1_g_ref, ln1_b_ref,
                w_route_ref, w_route_hi_ref, b_route_ref, h1_alias_ref, route_alias_ref,
                h1_ref, route_ref, rnew_ref, cnew_ref,
                r_scr, ubuf, c_scr,
                *, n_seq, seq_rows, state_decay):
    del h1_alias_ref, route_alias_ref
    t = pl.program_id(1)
    last_t = pl.num_programs(1) - 1
    bf16 = jnp.bfloat16

    @pl.when(t == 0)
    def _():
        r_scr[...] = rinit_ref[...]
        ubuf[:, 0:HIST_ROWS, :] = cinit_ref[...]

    h = _layer_norm(x_ref[...], ln_in_g_ref[...], ln_in_b_ref[...])
    hb = h.astype(bf16)

    u = _dot(hb, w_in_ref[:, C_GLU_A:C_GLU_A + D_MODEL]) * _sigmoid(
        _dot(hb, w_in_ref[:, C_GLU_B:C_GLU_B + D_MODEL]))
    for sq_i in range(n_seq):
        ubuf[sq_i, HIST_ROWS:HIST_ROWS + seq_rows, :] = u[sq_i * seq_rows:(sq_i + 1) * seq_rows]

    row_block = min(CONV_ROW_BLOCK, seq_rows)

    def conv_lane_chunk(j):
        lane = slice(j * LANES, (j + 1) * LANES)
        for sq_i in range(n_seq):
            for rb in range(seq_rows // row_block):
                r0 = rb * row_block
                acc = None
                for sub in range(SUBLANES):
                    taps = [(a, SUBLANES * a + sub - HIST_PAD) for a in range(HIST_ROWS // SUBLANES + 1)]
                    taps = [(a, kk) for a, kk in taps if 0 <= kk < CONV_WIDTH]
                    span = row_block + (SUBLANES if sub else 0)
                    partial = None
                    for a, kk in taps:
                        term = conv_w_ref[kk:kk + 1, lane] * ubuf[sq_i, r0 + SUBLANES * a:
                                                                    r0 + SUBLANES * a + span, lane]
                        partial = term if partial is None else partial + term
                    partial = partial[sub:sub + row_block]
                    acc = partial if acc is None else acc + partial
                c_scr[sq_i * seq_rows + r0:sq_i * seq_rows + r0 + row_block, lane] = acc

    assert D_MODEL // LANES == 2 * RET_HEADS

    q = _dot(hb, w_in_ref[:, C_Q:C_Q + RET_QK])
    conv_lane_chunk(0)
    k = _dot(hb, w_in_ref[:, C_K:C_K + RET_QK])
    conv_lane_chunk(1)
    v = _dot(hb, w_in_ref[:, C_V:C_V + RET_V])
    conv_lane_chunk(2)
    g = _dot(hb, w_in_ref[:, C_G:C_G + RET_V])
    g = g * _sigmoid(g)
    ga = _sigmoid(_dot(hb, w_in_ref[:, C_GA:C_GA + D_MODEL]))
    conv_lane_chunk(3)
    gb = _sigmoid(_dot(hb, w_in_ref[:, C_GB:C_GB + D_MODEL]))
    cq, sq, ck, sk = cq_ref[...], sq_ref[...], ck_ref[...], sk_ref[...]

    y_ret = None
    for hd in range(RET_HEADS):
        qh = q[:, hd * RET_DK:(hd + 1) * RET_DK]
        kh = k[:, hd * RET_DK:(hd + 1) * RET_DK]
        qr = qh * cq + pltpu.roll(qh, RET_DK // 2, 1) * sq
        kr = kh * ck + pltpu.roll(kh, RET_DK // 2, 1) * sk
        vb = v[:, hd * RET_DV:(hd + 1) * RET_DV].astype(bf16)
        s = lax.dot_general(qr.astype(bf16), kr.astype(bf16), (((1,), (1,)), ((), ())),
                            preferred_element_type=jnp.float32)
        inner = _dot((s * mask_ref[hd]).astype(bf16), vb)
        qx = (qr * xi_ref[hd]).astype(bf16)
        kz = (kr * zeta_ref[hd]).astype(bf16)
        o_parts = []
        for sq_i in range(n_seq):
            rows = slice(sq_i * seq_rows, (sq_i + 1) * seq_rows)
            r_old = r_scr[sq_i, hd]
            o_parts.append(inner[rows] + _dot(qx[rows], r_old.astype(bf16)))
            kv = lax.dot_general(kz[rows], vb[rows], (((0,), (0,)), ((), ())),
                                 preferred_element_type=jnp.float32)
            r_scr[sq_i, hd] = state_decay[hd] * r_old + kv
        o = o_parts[0] if n_seq == 1 else jnp.concatenate(o_parts, axis=0)
        on = _layer_norm(o, 1.0, 0.0)
        gated = (on * g[:, hd * RET_DV:(hd + 1) * RET_DV]).astype(bf16)
        part = _dot(gated, w_ret_out_ref[hd * RET_DV:(hd + 1) * RET_DV, :])
        y_ret = part if y_ret is None else y_ret + part
        conv_lane_chunk(RET_HEADS + hd)

    for sq_i in range(n_seq):
        ubuf[sq_i, 0:HIST_ROWS, :] = ubuf[sq_i, seq_rows:seq_rows + HIST_ROWS, :]

    c = _layer_norm(c_scr[...] + conv_b_ref[...], conv_ln_g_ref[...], conv_ln_b_ref[...])
    c = c * _sigmoid(c)
    y_conv = _dot(c.astype(bf16), w_conv_out_ref[...])

    merged =(ga * y_ret + gb * y_conv).astype(bf16)
    h1 = _layer_norm(DN_ALPHA * h + _dot(merged, w_out_ref[...]), ln1_g_ref[...], ln1_b_ref[...])
    _store_row_tiles(h1_ref, h1)

    h1_hi = h1.astype(bf16)
    h1_lo = (h1 - h1_hi.astype(jnp.float32)).astype(bf16)
    both = _dot(h1_hi, w_route_ref[...])
    logits = both[:, :LANES] + both[:, LANES:] + _dot(h1_lo, w_route_hi_ref[...]) + b_route_ref[...]
    lane = lax.broadcasted_iota(jnp.int32, logits.shape, 1)
    lane_f = lane.astype(jnp.float32)
    is_group = lane < N_GROUPS
    gl = jnp.where(is_group, logits, NEG_BIG)
    gmax = jnp.max(gl, axis=-1, keepdims=True)
    gexp = jnp.where(is_group, jnp.exp(gl - gmax), 0.0)
    gprob = gexp / jnp.sum(gexp, axis=-1, keepdims=True)
    gp = jnp.max(gprob, axis=-1, keepdims=True)
    gi = jnp.min(jnp.where(is_group & (gprob == gp), lane_f, float(LANES)), axis=-1, keepdims=True)
    e_lo = N_GROUPS + EXPERTS_PER_GROUP * gi
    in_group = (lane_f >= e_lo) & (lane_f < e_lo + EXPERTS_PER_GROUP)
    el = jnp.where(in_group, logits, NEG_BIG)
    ev1 = jnp.max(el, axis=-1, keepdims=True)
    ei1 = jnp.min(jnp.where(in_group & (el == ev1), lane_f, float(LANES)), axis=-1, keepdims=True)
    rest = in_group & (lane_f != ei1)
    el2 = jnp.where(rest, logits, NEG_BIG)
    ev2 = jnp.max(el2, axis=-1, keepdims=True)
    ei2 = jnp.min(jnp.where(rest & (el2 == ev2), lane_f, float(LANES)), axis=-1, keepdims=True)
    x2 = jnp.exp(ev2 - ev1)
    w1 = gp / (1.0 + x2)
    w2 = w1 * x2
    route = jnp.where(lane == 0, ei1 - N_GROUPS, 0.0)
    route = jnp.where(lane == 1, ei2 - N_GROUPS, route)
    route = jnp.where(lane == 2, w1, route)
    route = jnp.where(lane == 3, w2, route)
    route_ref[...] = route

    @pl.when(t == last_t)
    def _():
        rnew_ref[...] = r_scr[...]
        cnew_ref[...] = ubuf[:, 0:HIST_ROWS, :]


def _const_spec(shape):
    zeros = (0,) * len(shape)
    return pl.BlockSpec(shape, lambda b, t: zeros, pipeline_mode=pl.Buffered(1))


def _mix_call(x, rinit, cinit, tables, weights, h1_buf, route_buf, *, n_seq, seq_rows, n_groups, n_steps,
              table_per_step, tile_offset):
    tt = n_seq * seq_rows
    assert tt == TOKEN_TILE and x.shape[0] == n_groups * n_steps * tt
    log_decay = [math.log1p(-(2.0 ** (-5.0 - hd))) for hd in range(RET_HEADS)]
    state_decay = tuple(math.exp(lg * seq_rows) for lg in log_decay)
    cq, sq, ck, sk, xi, zeta, mask = tables

    in_tok_spec = pl.BlockSpec((tt, D_MODEL), lambda b, t: (b * n_steps + t, 0))
    out_tok_spec = lambda rows: pl.BlockSpec((rows, LANES), lambda b, t: (tile_offset + b * n_steps + t, 0))
    rope_spec = pl.BlockSpec((tt, LANES), (lambda b, t: (t, 0)) if table_per_step else (lambda b, t: (0, 0)))
    state_r_spec = pl.BlockSpec((n_seq, RET_HEADS, RET_DK, RET_DV), lambda b, t: (b, 0, 0, 0))
    state_c_spec = pl.BlockSpec((n_seq, HIST_ROWS, D_MODEL), lambda b, t: (b, 0, 0))
    any_spec = pl.BlockSpec(memory_space=pl.ANY)

    in_specs = [in_tok_spec, state_r_spec, state_c_spec, rope_spec, rope_spec, rope_spec, rope_spec,
                _const_spec(xi.shape), _const_spec(zeta.shape), _const_spec(mask.shape)]
    in_specs += [_const_spec(w.shape) for w in weights]
    in_specs += [any_spec, any_spec]
    n_in = len(in_specs)
    out_shape = (jax.ShapeDtypeStruct(h1_buf.shape, jnp.float32),
                 jax.ShapeDtypeStruct(route_buf.shape, jnp.float32),
                 jax.ShapeDtypeStruct(rinit.shape, jnp.float32),
                 jax.ShapeDtypeStruct(cinit.shape, jnp.float32))
    out_specs = (out_tok_spec(tt * ROW_TILE_ROWS), out_tok_spec(tt), state_r_spec, state_c_spec)
    body = functools.partial(_mix_kernel, n_seq=n_seq, seq_rows=seq_rows, state_decay=state_decay)
    return pl.pallas_call(
        body,
        grid=(n_groups, n_steps),
        in_specs=in_specs,
        out_specs=out_specs,
        out_shape=out_shape,
        input_output_aliases={n_in - 2: 0, n_in - 1: 1},
        scratch_shapes=[pltpu.VMEM((n_seq, RET_HEADS, RET_DK, RET_DV), jnp.float32),
                        pltpu.VMEM((n_seq, HIST_ROWS + seq_rows, D_MODEL), jnp.float32),
                        pltpu.VMEM((tt, D_MODEL), jnp.float32)],
        compiler_params=pltpu.CompilerParams(dimension_semantics=("arbitrary", "arbitrary"),
                                             vmem_limit_bytes=VMEM_LIMIT),
        name="mix",
    )(x, rinit, cinit, cq, sq, ck, sk, xi, zeta, mask, *weights, h1_buf, route_buf)


def _mix_tables(positions, n_seq, seq_rows):
    half = RET_DK // 2
    inv = ROPE_BASE ** (-np.arange(half, dtype=np.float64) / half)
    ang = np.asarray(positions, np.float64)[:, None] * inv[None, :]
    cos, sin = np.cos(ang), np.sin(ang)
    cq = np.concatenate([cos, cos], axis=1)
    sq = np.concatenate([-sin, sin], axis=1)
    scale = RET_DK ** -0.5
    lg = np.log1p(-np.exp2(-5.0 - np.arange(RET_HEADS, dtype=np.float64)))
    tt = n_seq * seq_rows
    i = np.arange(tt)
    loc = (i % seq_rows).astype(np.float64)
    xi = np.exp(lg[:, None] * (loc + 1.0)[None, :])
    zeta = np.exp(lg[:, None] * (seq_rows - 1.0 - loc)[None, :])
    xi = np.broadcast_to(xi[:, :, None], (RET_HEADS, tt, LANES))
    zeta = np.broadcast_to(zeta[:, :, None], (RET_HEADS, tt, LANES))
    same_seq = (i[:, None] // seq_rows) == (i[None, :] // seq_rows)
    visible = same_seq & ((i[None, :] // CHUNK) <= (i[:, None] // CHUNK))
    dist = np.abs(i[:, None] - i[None, :]).astype(np.float64)
    mask = np.where(visible[None], np.exp(lg[:, None, None] * dist[None]), 0.0)
    tables = (cq, sq, cq * scale, sq * scale, xi, zeta, mask)
    return tuple(jnp.asarray(np.ascontiguousarray(tab), jnp.float32) for tab in tables)


def _expert_kernel(tile_expert_ref, n_used_ref, row_token_ref, h1_hbm, wg_ref, wu_ref, wd_ref, ys_ref,
                   xbuf, sems, wg_b, wu_b, wd_b):
    i = pl.program_id(0)
    n_used = n_used_ref[0]
    bf16 = jnp.bfloat16
    slot = lax.rem(i, 2)

    def start_gather(tile, dst_slot):
        def issue(r, carry):
            src = pl.multiple_of(row_token_ref[tile * EXPERT_TILE + r] * ROW_TILE_ROWS, ROW_TILE_ROWS)
            dst = pl.multiple_of(r * ROW_TILE_ROWS, ROW_TILE_ROWS)
            pltpu.make_async_copy(h1_hbm.at[pl.ds(src, ROW_TILE_ROWS)],
                                  xbuf.at[dst_slot, pl.ds(dst, ROW_TILE_ROWS)],
                                  sems.at[dst_slot]).start(priority=1)
            return carry
        lax.fori_loop(0, EXPERT_TILE, issue, 0, unroll=8)

    def wait_gather(dst_slot):
        pltpu.make_async_copy(h1_hbm.at[pl.ds(0, EXPERT_TILE * ROW_TILE_ROWS)], xbuf.at[dst_slot],
                              sems.at[dst_slot]).wait()

    @pl.when(i == 0)
    def _():
        start_gather(0, 0)

    @pl.when(i + 1 < n_used)
    def _():
        start_gather(i + 1, 1 - slot)

    prev = tile_expert_ref[jnp.maximum(i - 1, 0)]
    new_expert = (i == 0) | (tile_expert_ref[i] != prev)

    @pl.when(new_expert)
    def _():
        wg_b[...] = wg_ref[0].astype(bf16)
        wu_b[...] = wu_ref[0].astype(bf16)
        wd_b[...] = wd_ref[0].astype(bf16)

    @pl.when(i < n_used)
    def _():
        wait_gather(slot)
        x = _load_row_tiles(xbuf.at[slot], EXPERT_TILE).astype(bf16)
        gate = _dot(x, wg_b[...])
        up = _dot(x, wu_b[...])
        act = (gate * _sigmoid(gate) * up).astype(bf16)
        _store_row_tiles(ys_ref, _dot(act, wd_b[...]))

    @pl.when(i >= n_used)
    def _():
        ys_ref[...] = jnp.zeros_like(ys_ref)


def _expert_call(tile_expert, n_used, row_token, h1, wg, wu, wd):
    n_tiles = tile_expert.shape[0]
    w_map = lambda i, te, nu, rt: (te[i], 0, 0)
    grid_spec = pltpu.PrefetchScalarGridSpec(
        num_scalar_prefetch=3,
        grid=(n_tiles,),
        in_specs=[pl.BlockSpec(memory_space=pl.ANY),
                  pl.BlockSpec((1, D_MODEL, EXPERT_FF), w_map),
                  pl.BlockSpec((1, D_MODEL, EXPERT_FF), w_map),
                  pl.BlockSpec((1, EXPERT_FF, D_MODEL), w_map)],
        out_specs=pl.BlockSpec((EXPERT_TILE * ROW_TILE_ROWS, LANES), lambda i, te, nu, rt: (i, 0)),
        scratch_shapes=[pltpu.VMEM((2, EXPERT_TILE * ROW_TILE_ROWS, LANES), jnp.float32),
                        pltpu.SemaphoreType.DMA((2,)),
                        pltpu.VMEM((D_MODEL, EXPERT_FF), jnp.bfloat16),
                        pltpu.VMEM((D_MODEL, EXPERT_FF), jnp.bfloat16),
                        pltpu.VMEM((EXPERT_FF, D_MODEL), jnp.bfloat16)],
    )
    return pl.pallas_call(
        _expert_kernel,
        grid_spec=grid_spec,
        out_shape=jax.ShapeDtypeStruct((n_tiles * EXPERT_TILE * ROW_TILE_ROWS, LANES), jnp.float32),
        compiler_params=pltpu.CompilerParams(dimension_semantics=("arbitrary",),
                                             vmem_limit_bytes=VMEM_LIMIT),
        name="experts",
    )(tile_expert, n_used, row_token, h1, wg, wu, wd)


def _combine_kernel(pos_ref, h1_ref, route_ref, pp_ref, ps_ref, ys_hbm, w_ple_proj_ref, w_ple_gate_ref,
                    ln2_g_ref, ln2_b_ref, yp_ref, ys_out_ref, ybuf, sems, *, n_prompt_tiles):
    i = pl.program_id(0)
    n_steps = pl.num_programs(0)
    bf16 = jnp.bfloat16
    slot = lax.rem(i, 2)

    def start_gather(tile, dst_slot):
        def issue(r, carry):
            pair = 2 * (tile * TOKEN_TILE + r)
            dst = pl.multiple_of(r * ROW_TILE_ROWS, ROW_TILE_ROWS)
            for which in range(2):
                src = pl.multiple_of(pos_ref[pair + which] * ROW_TILE_ROWS, ROW_TILE_ROWS)
                pltpu.make_async_copy(ys_hbm.at[pl.ds(src, ROW_TILE_ROWS)],
                                      ybuf.at[dst_slot, which, pl.ds(dst, ROW_TILE_ROWS)],
                                      sems.at[dst_slot]).start(priority=which)
            return carry
        lax.fori_loop(0, TOKEN_TILE, issue, 0, unroll=4)

    def wait_gather(dst_slot):
        for which in range(2):
            pltpu.make_async_copy(ys_hbm.at[pl.ds(0, TOKEN_TILE * ROW_TILE_ROWS)], ybuf.at[dst_slot, which],
                                  sems.at[dst_slot]).wait()

    @pl.when(i == 0)
    def _():
        start_gather(0, 0)

    @pl.when(i + 1 < n_steps)
    def _():
        start_gather(i + 1, 1 - slot)

    h1 = _load_row_tiles(h1_ref, TOKEN_TILE)
    is_prompt = i < n_prompt_tiles
    p = jnp.where(is_prompt, pp_ref[...], ps_ref[...])
    pe = _dot(p.astype(bf16), w_ple_proj_ref[...]) * _sigmoid(_dot(h1.astype(bf16), w_ple_gate_ref[...]))
    route = route_ref[...]
    w1 = route[:, 2:3]
    w2 = route[:, 3:4]
    wait_gather(slot)
    y1 = _load_row_tiles(ybuf.at[slot, 0], TOKEN_TILE)
    y2 = _load_row_tiles(ybuf.at[slot, 1], TOKEN_TILE)
    total = DN_ALPHA * h1 + pe + w1 * y1 + w2 * y2
    out = _layer_norm(total, ln2_g_ref[...], ln2_b_ref[...])

    @pl.when(is_prompt)
    def _():
        yp_ref[...] = out

    @pl.when(jnp.logical_not(is_prompt))
    def _():
        ys_out_ref[...] = out


def _combine_call(pos, h1, route, p_prompt, p_sample, ys, w_ple_proj, w_ple_gate, ln2_g, ln2_b):
    n_tok = route.shape[0]
    n_prompt_tiles = p_prompt.shape[0] // TOKEN_TILE
    n_sample_tiles = p_sample.shape[0] // TOKEN_TILE
    n_steps = n_tok // TOKEN_TILE
    assert n_steps == n_prompt_tiles + n_sample_tiles
    tok_spec = lambda cols: pl.BlockSpec((TOKEN_TILE, cols), lambda i, pos: (i, 0))
    prompt_spec = lambda cols: pl.BlockSpec((TOKEN_TILE, cols),
                                            lambda i, pos: (jnp.minimum(i, n_prompt_tiles - 1), 0))
    sample_spec = lambda cols: pl.BlockSpec((TOKEN_TILE, cols),
                                            lambda i, pos: (jnp.maximum(i - n_prompt_tiles, 0), 0))
    const_spec = lambda shape: pl.BlockSpec(shape, lambda i, pos: (0, 0))
    grid_spec = pltpu.PrefetchScalarGridSpec(
        num_scalar_prefetch=1,
        grid=(n_steps,),
        in_specs=[pl.BlockSpec((TOKEN_TILE * ROW_TILE_ROWS, LANES), lambda i, pos: (i, 0)), tok_spec(LANES),
                  prompt_spec(PLE_DIM), sample_spec(PLE_DIM),
                  pl.BlockSpec(memory_space=pl.ANY), const_spec(w_ple_proj.shape), const_spec(w_ple_gate.shape),
                  const_spec(ln2_g.shape), const_spec(ln2_b.shape)],
        out_specs=(prompt_spec(D_MODEL), sample_spec(D_MODEL)),
        scratch_shapes=[pltpu.VMEM((2, 2, TOKEN_TILE * ROW_TILE_ROWS, LANES), jnp.float32),
                        pltpu.SemaphoreType.DMA((2,))],
    )
    return pl.pallas_call(
        functools.partial(_combine_kernel, n_prompt_tiles=n_prompt_tiles),
        grid_spec=grid_spec,
        out_shape=(jax.ShapeDtypeStruct((n_prompt_tiles * TOKEN_TILE, D_MODEL), jnp.float32),
                   jax.ShapeDtypeStruct((n_sample_tiles * TOKEN_TILE, D_MODEL), jnp.float32)),
        compiler_params=pltpu.CompilerParams(dimension_semantics=("arbitrary",),
                                             vmem_limit_bytes=VMEM_LIMIT),
        name="combine",
    )(pos, h1, route, p_prompt, p_sample, ys, w_ple_proj, w_ple_gate, ln2_g, ln2_b)


def _dispatch_plan(route, n_tiles):
    experts = route[:, :2].astype(jnp.int32).reshape(-1)
    n_pairs = experts.shape[0]
    onehot = (experts[:, None] == jnp.arange(N_EXPERTS, dtype=jnp.int32)[None, :]).astype(jnp.int32)
    running = jnp.cumsum(onehot, axis=0)
    rank = jnp.sum(running * onehot, axis=1) - 1
    counts = running[-1]
    tiles_per = (counts + EXPERT_TILE - 1) // EXPERT_TILE
    tile_end = jnp.cumsum(tiles_per)
    row_start = (tile_end - tiles_per) * EXPERT_TILE
    pos = row_start[experts] + rank
    n_used = tile_end[-1]
    tile_ids = jnp.minimum(jnp.arange(n_tiles, dtype=jnp.int32), n_used - 1)
    tile_expert = jnp.sum((tile_ids[:, None] >= tile_end[None, :]).astype(jnp.int32), axis=1)

    order = jnp.argsort(experts, stable=True).astype(jnp.int32)
    pair_start = jnp.cumsum(counts) - counts
    rows = jnp.arange(n_tiles * EXPERT_TILE, dtype=jnp.int32)
    row_expert = jnp.repeat(tile_expert, EXPERT_TILE)
    k = rows - row_start[row_expert]
    valid = k < counts[row_expert]
    src_pair = order[jnp.clip(pair_start[row_expert] + k, 0, n_pairs - 1)]
    row_token = jnp.where(valid, src_pair // 2, 0)
    return (pos.astype(jnp.int32), row_token.astype(jnp.int32), tile_expert.astype(jnp.int32),
            n_used.reshape(1).astype(jnp.int32))


def kernel(x_prompt, x_sample, p_prompt, p_sample, state_ret, state_conv, ln_in_g, ln_in_b, w_in, w_ret_out,
           conv_w, conv_b, conv_ln_g, conv_ln_b, w_conv_out, w_out, ln1_g, ln1_b, w_route_g, b_route_g,
           w_route_e, b_route_e, w_exp_gate, w_exp_up, w_exp_down, w_ple_proj, w_ple_gate, ln2_g, ln2_b):
    assert w_in.shape[0] == DEPTH == 1
    bf16 = jnp.bfloat16
    batch, seq, _ = x_prompt.shape
    dec_batch, dec_seq, _ = x_sample.shape
    assert dec_seq == CHUNK and seq % TOKEN_TILE == 0 and TOKEN_TILE % CHUNK == 0
    row = lambda a: a.reshape(1, -1)
    n_prompt = batch * seq
    n_sample = dec_batch * dec_seq
    n_tok = n_prompt + n_sample

    w_route = jnp.concatenate(
        [w_route_g[0], jnp.transpose(w_route_e[0], (1, 0, 2)).reshape(D_MODEL, N_EXPERTS),
         jnp.zeros((D_MODEL, LANES - ROUTE_COLS), jnp.float32)], axis=1)
    w_route_hi = w_route.astype(bf16)
    w_route_lo = (w_route - w_route_hi.astype(jnp.float32)).astype(bf16)
    b_route = jnp.concatenate([b_route_g[0], b_route_e[0].reshape(-1),
                               jnp.zeros((LANES - ROUTE_COLS,), jnp.float32)]).reshape(1, LANES)
    conv_w_pad = jnp.concatenate([conv_w[0], jnp.zeros((HIST_ROWS - CONV_WIDTH, D_MODEL), jnp.float32)], axis=0)
    weights = (row(ln_in_g), row(ln_in_b), w_in[0].astype(bf16), w_ret_out[0].astype(bf16), conv_w_pad,
               row(conv_b[0]), row(conv_ln_g[0]), row(conv_ln_b[0]), w_conv_out[0].astype(bf16),
               w_out[0].astype(bf16), row(ln1_g[0]), row(ln1_b[0]),
               jnp.concatenate([w_route_hi, w_route_lo], axis=1), w_route_hi, b_route)

    h1 = jnp.zeros((n_tok * ROW_TILE_ROWS, LANES), jnp.float32)
    route = jnp.zeros((n_tok, LANES), jnp.float32)

    tables_p = _mix_tables(np.arange(seq), 1, TOKEN_TILE)
    rinit_p = jnp.zeros((batch, RET_HEADS, RET_DK, RET_DV), jnp.float32)
    cinit_p = jnp.zeros((batch, HIST_ROWS, D_MODEL), jnp.float32)
    h1, route, rnew_p, cnew_p = _mix_call(
        x_prompt.reshape(n_prompt, D_MODEL), rinit_p, cinit_p, tables_p, weights, h1, route,
        n_seq=1, seq_rows=TOKEN_TILE, n_groups=batch, n_steps=seq // TOKEN_TILE, table_per_step=True,
        tile_offset=0)

    seqs_per_tile = TOKEN_TILE // dec_seq
    assert dec_batch % seqs_per_tile == 0
    pos_s = np.tile(PAST_LEN + np.arange(dec_seq), seqs_per_tile)
    tables_s = _mix_tables(pos_s, seqs_per_tile, dec_seq)
    cinit_s = jnp.pad(state_conv[0], ((0, 0), (HIST_PAD, 0), (0, 0)))
    h1, route, rnew_s, cnew_s = _mix_call(
        x_sample.reshape(n_sample, D_MODEL), state_ret[0], cinit_s, tables_s, weights, h1, route,
        n_seq=seqs_per_tile, seq_rows=dec_seq, n_groups=dec_batch // seqs_per_tile, n_steps=1,
        table_per_step=False, tile_offset=n_prompt // TOKEN_TILE)

    n_tiles = (2 * n_tok) // EXPERT_TILE + N_EXPERTS
    pos, row_token, tile_expert, n_used = _dispatch_plan(route, n_tiles)
    ys = _expert_call(tile_expert, n_used, row_token, h1,
                      w_exp_gate[0].reshape(N_EXPERTS, D_MODEL, EXPERT_FF),
                      w_exp_up[0].reshape(N_EXPERTS, D_MODEL, EXPERT_FF),
                      w_exp_down[0].reshape(N_EXPERTS, EXPERT_FF, D_MODEL))
    y_p, y_s = _combine_call(pos, h1, route, p_prompt[0].reshape(n_prompt, PLE_DIM),
                             p_sample[0].reshape(n_sample, PLE_DIM), ys, w_ple_proj[0].astype(bf16),
                             w_ple_gate[0].astype(bf16), row(ln2_g[0]), row(ln2_b[0]))

    return (y_p.reshape(batch, seq, D_MODEL), y_s.reshape(dec_batch, dec_seq, D_MODEL), rnew_p[None],
            cnew_p[None, :, HIST_PAD:, :], rnew_s[None], cnew_s[None, :, HIST_PAD:, :])
```

```python
import functools
import math

import jax
import jax.numpy as jnp
import numpy as np
from jax import lax
from jax.experimental import pallas as pl
from jax.experimental.pallas import tpu as pltpu

D_MODEL = 1024
RET_HEADS = 4
RET_DK = 128
RET_DV = 256
RET_QK = RET_HEADS * RET_DK
RET_V = RET_HEADS * RET_DV
CHUNK = 64
CONV_WIDTH = 31
CONV_HIST = CONV_WIDTH - 1
HIST_ROWS = 32
HIST_PAD = HIST_ROWS - CONV_HIST
N_GROUPS = 4
EXPERTS_PER_GROUP = 8
N_EXPERTS = N_GROUPS * EXPERTS_PER_GROUP
EXPERT_FF = 512
PLE_DIM = 256
PAST_LEN = 4096
LN_EPS = 1e-5
ROPE_BASE = 10000.0
DEPTH = 1
DN_ALPHA = float((2 * DEPTH) ** 0.25)

LANES = 128
SUBLANES = 8
TOKEN_TILE = 256
EXPERT_TILE = 256
GATHER_SLOTS = 3
ROW_TILE_ROWS = D_MODEL // LANES
CONV_ROW_BLOCK = 64
VMEM_LIMIT = 56 * 1024 * 1024

C_Q = 0
C_K = RET_QK
C_V = 2 * RET_QK
C_G = C_V + RET_V
C_GLU_A = C_G + RET_V
C_GLU_B = C_GLU_A + D_MODEL
C_GA = C_GLU_B + D_MODEL
C_GB = C_GA + D_MODEL

ROUTE_COLS = N_GROUPS + N_EXPERTS
NEG_BIG = -1e30
NEG_LOG2_E = -math.log2(math.e)


def _sigmoid(x):
    return 1.0 / (1.0 + jnp.exp2(x * NEG_LOG2_E))


def _layer_norm(x, g, b):
    mu = jnp.mean(x, axis=-1, keepdims=True)
    xc = x - mu
    var = jnp.mean(xc * xc, axis=-1, keepdims=True)
    return xc * lax.rsqrt(var + LN_EPS) * g + b


def _dot(a, b):
    return jnp.dot(a, b, preferred_element_type=jnp.float32)


def _store_row_tiles(ref, x):
    n = x.shape[0]
    for j in range(ROW_TILE_ROWS):
        ref[pl.ds(j, n, stride=ROW_TILE_ROWS), :] = x[:, j * LANES:(j + 1) * LANES]


def _load_row_tiles(ref, n):
    return jnp.concatenate([ref[pl.ds(j, n, stride=ROW_TILE_ROWS), :] for j in range(ROW_TILE_ROWS)], axis=1)


N_MIX_INPUTS = 25


def _mix_kernel(*refs, n_real_groups, n_fill_groups, aliased, **tile_params):
    if aliased:
        refs = refs[:N_MIX_INPUTS] + refs[N_MIX_INPUTS + 2:]
    h1_ref, route_ref = refs[N_MIX_INPUTS], refs[N_MIX_INPUTS + 1]
    if n_fill_groups == 0:
        _mix_tile(*refs, **tile_params)
        return
    group = pl.program_id(0)

    @pl.when(group < n_real_groups)
    def _():
        _mix_tile(*refs, **tile_params)

    @pl.when(group >= n_real_groups)
    def _():
        h1_ref[...] = jnp.zeros_like(h1_ref)
        route_ref[...] = jnp.zeros_like(route_ref)


def _mix_tile(x_ref, rinit_ref, cinit_ref, cq_ref, sq_ref, ck_ref, sk_ref,
              xi_ref, zeta_ref, mask_ref,
              ln_in_g_ref, ln_in_b_ref, w_in_ref, w_ret_out_ref, conv_w_ref, conv_b_ref,
              conv_ln_g_ref, conv_ln_b_ref, w_conv_out_ref, w_out_ref, ln1_g_ref, ln1_b_ref,
              w_route_ref, w_route_hi_ref, b_route_ref,
              h1_ref, route_ref, rnew_ref, cnew_ref,
              r_scr, ubuf, c_scr,
              *, n_seq, seq_rows, state_decay):
    t = pl.program_id(1)
    last_t = pl.num_programs(1) - 1
    bf16 = jnp.bfloat16

    @pl.when(t == 0)
    def _():
        r_scr[...] = rinit_ref[...]
        ubuf[:, 0:HIST_ROWS, :] = cinit_ref[...]

    h = _layer_norm(x_ref[...], ln_in_g_ref[...], ln_in_b_ref[...])
    hb = h.astype(bf16)

    u = _dot(hb, w_in_ref[:, C_GLU_A:C_GLU_A + D_MODEL]) * _sigmoid(
        _dot(hb, w_in_ref[:, C_GLU_B:C_GLU_B + D_MODEL]))
    for sq_i in range(n_seq):
        ubuf[sq_i, HIST_ROWS:HIST_ROWS + seq_rows, :] = u[sq_i * seq_rows:(sq_i + 1) * seq_rows]

    row_block = min(CONV_ROW_BLOCK, seq_rows)

    def conv_lane_chunk(j):
        lane = slice(j * LANES, (j + 1) * LANES)
        for sq_i in range(n_seq):
            for rb in range(seq_rows // row_block):
                r0 = rb * row_block
                acc = None
                for sub in range(SUBLANES):
                    taps = [(a, SUBLANES * a + sub - HIST_PAD) for a in range(HIST_ROWS // SUBLANES + 1)]
                    taps = [(a, kk) for a, kk in taps if 0 <= kk < CONV_WIDTH]
                    span = row_block + (SUBLANES if sub else 0)
                    partial = None
                    for a, kk in taps:
                        term = conv_w_ref[kk:kk + 1, lane] * ubuf[sq_i, r0 + SUBLANES * a:
                                                                    r0 + SUBLANES * a + span, lane]
                        partial = term if partial is None else partial + term
                    partial = partial[sub:sub + row_block]
                    acc = partial if acc is None else acc + partial
                c_scr[sq_i * seq_rows + r0:sq_i * seq_rows + r0 + row_block, lane] = acc

    assert D_MODEL // LANES == 2 * RET_HEADS

    q = _dot(hb, w_in_ref[:, C_Q:C_Q + RET_QK])
    conv_lane_chunk(0)
    k = _dot(hb, w_in_ref[:, C_K:C_K + RET_QK])
    conv_lane_chunk(1)
    v = _dot(hb, w_in_ref[:, C_V:C_V + RET_V])
    conv_lane_chunk(2)
    g = _dot(hb, w_in_ref[:, C_G:C_G + RET_V])
    g = g * _sigmoid(g)
    ga = _sigmoid(_dot(hb, w_in_ref[:, C_GA:C_GA + D_MODEL]))
    conv_lane_chunk(3)
    gb = _sigmoid(_dot(hb, w_in_ref[:, C_GB:C_GB + D_MODEL]))
    cq, sq, ck, sk = cq_ref[...], sq_ref[...], ck_ref[...], sk_ref[...]

    y_ret = None
    for hd in range(RET_HEADS):
        qh = q[:, hd * RET_DK:(hd + 1) * RET_DK]
        kh = k[:, hd * RET_DK:(hd + 1) * RET_DK]
        qr = qh * cq + pltpu.roll(qh, RET_DK // 2, 1) * sq
        kr = kh * ck + pltpu.roll(kh, RET_DK // 2, 1) * sk
        vb = v[:, hd * RET_DV:(hd + 1) * RET_DV].astype(bf16)
        s = lax.dot_general(qr.astype(bf16), kr.astype(bf16), (((1,), (1,)), ((), ())),
                            preferred_element_type=jnp.float32)
        inner = _dot((s * mask_ref[hd]).astype(bf16), vb)
        qx = (qr * xi_ref[hd]).astype(bf16)
        kz = (kr * zeta_ref[hd]).astype(bf16)
        o_parts = []
        for sq_i in range(n_seq):
            rows = slice(sq_i * seq_rows, (sq_i + 1) * seq_rows)
            r_old = r_scr[sq_i, hd]
            o_parts.append(inner[rows] + _dot(qx[rows], r_old.astype(bf16)))
            kv = lax.dot_general(kz[rows], vb[rows], (((0,), (0,)), ((), ())),
                                 preferred_element_type=jnp.float32)
            r_scr[sq_i, hd] = state_decay[hd] * r_old + kv
        o = o_parts[0] if n_seq == 1 else jnp.concatenate(o_parts, axis=0)
        on = _layer_norm(o, 1.0, 0.0)
        gated = (on * g[:, hd * RET_DV:(hd + 1) * RET_DV]).astype(bf16)
        part = _dot(gated, w_ret_out_ref[hd * RET_DV:(hd + 1) * RET_DV, :])
        y_ret = part if y_ret is None else y_ret + part
        conv_lane_chunk(RET_HEADS + hd)

    for sq_i in range(n_seq):
        ubuf[sq_i, 0:HIST_ROWS, :] = ubuf[sq_i, seq_rows:seq_rows + HIST_ROWS, :]

    c = _layer_norm(c_scr[...] + conv_b_ref[...], conv_ln_g_ref[...], conv_ln_b_ref[...])
    c = c * _sigmoid(c)
    y_conv = _dot(c.astype(bf16), w_conv_out_ref[...])

    merged =(ga * y_ret + gb * y_conv).astype(bf16)
    h1 = _layer_norm(DN_ALPHA * h + _dot(merged, w_out_ref[...]), ln1_g_ref[...], ln1_b_ref[...])
    _store_row_tiles(h1_ref, h1)

    h1_hi = h1.astype(bf16)
    h1_lo = (h1 - h1_hi.astype(jnp.float32)).astype(bf16)
    both = _dot(h1_hi, w_route_ref[...])
    logits = both[:, :LANES] + both[:, LANES:] + _dot(h1_lo, w_route_hi_ref[...]) + b_route_ref[...]
    lane = lax.broadcasted_iota(jnp.int32, logits.shape, 1)
    lane_f = lane.astype(jnp.float32)
    is_group = lane < N_GROUPS
    gl = jnp.where(is_group, logits, NEG_BIG)
    gmax = jnp.max(gl, axis=-1, keepdims=True)
    gexp = jnp.where(is_group, jnp.exp(gl - gmax), 0.0)
    gprob = gexp / jnp.sum(gexp, axis=-1, keepdims=True)
    gp = jnp.max(gprob, axis=-1, keepdims=True)
    gi = jnp.min(jnp.where(is_group & (gprob == gp), lane_f, float(LANES)), axis=-1, keepdims=True)
    e_lo = N_GROUPS + EXPERTS_PER_GROUP * gi
    in_group = (lane_f >= e_lo) & (lane_f < e_lo + EXPERTS_PER_GROUP)
    el = jnp.where(in_group, logits, NEG_BIG)
    ev1 = jnp.max(el, axis=-1, keepdims=True)
    ei1 = jnp.min(jnp.where(in_group & (el == ev1), lane_f, float(LANES)), axis=-1, keepdims=True)
    rest = in_group & (lane_f != ei1)
    el2 = jnp.where(rest, logits, NEG_BIG)
    ev2 = jnp.max(el2, axis=-1, keepdims=True)
    ei2 = jnp.min(jnp.where(rest & (el2 == ev2), lane_f, float(LANES)), axis=-1, keepdims=True)
    x2 = jnp.exp(ev2 - ev1)
    w1 = gp / (1.0 + x2)
    w2 = w1 * x2
    route = jnp.where(lane == 0, ei1 - N_GROUPS, 0.0)
    route = jnp.where(lane == 1, ei2 - N_GROUPS, route)
    route = jnp.where(lane == 2, w1, route)
    route = jnp.where(lane == 3, w2, route)
    route_ref[...] = route

    @pl.when(t == last_t)
    def _():
        rnew_ref[...] = r_scr[...]
        cnew_ref[...] = ubuf[:, 0:HIST_ROWS, :]


def _const_spec(shape):
    zeros = (0,) * len(shape)
    return pl.BlockSpec(shape, lambda b, t: zeros, pipeline_mode=pl.Buffered(1))


def _mix_call(x, rinit, cinit, tables, weights, shared, *, n_tok_total, n_seq, seq_rows, n_groups, n_steps,
              table_per_step, tile_offset, n_fill_groups):
    tt = n_seq * seq_rows
    assert tt == TOKEN_TILE and x.shape[0] == n_groups * n_steps * tt
    log_decay = [math.log1p(-(2.0 ** (-5.0 - hd))) for hd in range(RET_HEADS)]
    state_decay = tuple(math.exp(lg * seq_rows) for lg in log_decay)
    cq, sq, ck, sk, xi, zeta, mask = tables
    last_group = n_groups - 1
    own = lambda b: jnp.minimum(b, last_group)

    in_tok_spec = pl.BlockSpec((tt, D_MODEL), lambda b, t: (own(b) * n_steps + t, 0))
    out_tok_spec = lambda rows: pl.BlockSpec((rows, LANES), lambda b, t: (tile_offset + b * n_steps + t, 0))
    rope_spec = pl.BlockSpec((tt, LANES), (lambda b, t: (t, 0)) if table_per_step else (lambda b, t: (0, 0)))
    state_r_spec = pl.BlockSpec((n_seq, RET_HEADS, RET_DK, RET_DV), lambda b, t: (own(b), 0, 0, 0))
    state_c_spec = pl.BlockSpec((n_seq, HIST_ROWS, D_MODEL), lambda b, t: (own(b), 0, 0))
    any_spec = pl.BlockSpec(memory_space=pl.ANY)

    in_specs = [in_tok_spec, state_r_spec, state_c_spec, rope_spec, rope_spec, rope_spec, rope_spec,
                _const_spec(xi.shape), _const_spec(zeta.shape), _const_spec(mask.shape)]
    in_specs += [_const_spec(w.shape) for w in weights]
    assert len(in_specs) == N_MIX_INPUTS
    aliases = {}
    extra = ()
    if shared is not None:
        assert n_fill_groups == 0
        in_specs += [any_spec, any_spec]
        aliases = {N_MIX_INPUTS: 0, N_MIX_INPUTS + 1: 1}
        extra = tuple(shared)
    out_shape = (jax.ShapeDtypeStruct((n_tok_total * ROW_TILE_ROWS, LANES), jnp.float32),
                 jax.ShapeDtypeStruct((n_tok_total, LANES), jnp.float32),
                 jax.ShapeDtypeStruct(rinit.shape, jnp.float32),
                 jax.ShapeDtypeStruct(cinit.shape, jnp.float32))
    out_specs = (out_tok_spec(tt * ROW_TILE_ROWS), out_tok_spec(tt), state_r_spec, state_c_spec)
    body = functools.partial(_mix_kernel, n_real_groups=n_groups, n_fill_groups=n_fill_groups,
                             aliased=shared is not None,
                             n_seq=n_seq, seq_rows=seq_rows, state_decay=state_decay)
    return pl.pallas_call(
        body,
        grid=(n_groups + n_fill_groups, n_steps),
        in_specs=in_specs,
        out_specs=out_specs,
        out_shape=out_shape,
        input_output_aliases=aliases,
        scratch_shapes=[pltpu.VMEM((n_seq, RET_HEADS, RET_DK, RET_DV), jnp.float32),
                        pltpu.VMEM((n_seq, HIST_ROWS + seq_rows, D_MODEL), jnp.float32),
                        pltpu.VMEM((tt, D_MODEL), jnp.float32)],
        compiler_params=pltpu.CompilerParams(dimension_semantics=("arbitrary", "arbitrary"),
                                             vmem_limit_bytes=VMEM_LIMIT),
        name="mix",
    )(x, rinit, cinit, cq, sq, ck, sk, xi, zeta, mask, *weights, *extra)


def _mix_tables(positions, n_seq, seq_rows):
    half = RET_DK // 2
    inv = ROPE_BASE ** (-np.arange(half, dtype=np.float64) / half)
    ang = np.asarray(positions, np.float64)[:, None] * inv[None, :]
    cos, sin = np.cos(ang), np.sin(ang)
    cq = np.concatenate([cos, cos], axis=1)
    sq = np.concatenate([-sin, sin], axis=1)
    scale = RET_DK ** -0.5
    lg = np.log1p(-np.exp2(-5.0 - np.arange(RET_HEADS, dtype=np.float64)))
    tt = n_seq * seq_rows
    i = np.arange(tt)
    loc = (i % seq_rows).astype(np.float64)
    xi = np.exp(lg[:, None] * (loc + 1.0)[None, :])
    zeta = np.exp(lg[:, None] * (seq_rows - 1.0 - loc)[None, :])
    xi = np.broadcast_to(xi[:, :, None], (RET_HEADS, tt, LANES))
    zeta = np.broadcast_to(zeta[:, :, None], (RET_HEADS, tt, LANES))
    same_seq = (i[:, None] // seq_rows) == (i[None, :] // seq_rows)
    visible = same_seq & ((i[None, :] // CHUNK) <= (i[:, None] // CHUNK))
    dist = np.abs(i[:, None] - i[None, :]).astype(np.float64)
    mask = np.where(visible[None], np.exp(lg[:, None, None] * dist[None]), 0.0)
    tables = (cq, sq, cq * scale, sq * scale, xi, zeta, mask)
    return tuple(jnp.asarray(np.ascontiguousarray(tab), jnp.float32) for tab in tables)


def _expert_kernel(tile_expert_ref, n_used_ref, row_token_ref, h1_hbm, wg_ref, wu_ref, wd_ref, ys_ref,
                   xbuf, sems, wg_b, wu_b, wd_b):
    i = pl.program_id(0)
    n_used = n_used_ref[0]
    bf16 = jnp.bfloat16
    slot = lax.rem(i, GATHER_SLOTS)

    def start_gather(tile):
        dst_slot = lax.rem(tile, GATHER_SLOTS)

        def issue(r, carry):
            src = pl.multiple_of(row_token_ref[tile * EXPERT_TILE + r] * ROW_TILE_ROWS, ROW_TILE_ROWS)
            dst = pl.multiple_of(r * ROW_TILE_ROWS, ROW_TILE_ROWS)
            pltpu.make_async_copy(h1_hbm.at[pl.ds(src, ROW_TILE_ROWS)],
                                  xbuf.at[dst_slot, pl.ds(dst, ROW_TILE_ROWS)],
                                  sems.at[dst_slot]).start(priority=1)
            return carry
        lax.fori_loop(0, EXPERT_TILE, issue, 0, unroll=8)

    def wait_gather(dst_slot):
        pltpu.make_async_copy(h1_hbm.at[pl.ds(0, EXPERT_TILE * ROW_TILE_ROWS)], xbuf.at[dst_slot],
                              sems.at[dst_slot]).wait()

    for first in range(GATHER_SLOTS - 1):
        @pl.when((i == 0) & (first < n_used))
        def _(first=first):
            start_gather(first)

    @pl.when(i + GATHER_SLOTS - 1 < n_used)
    def _():
        start_gather(i + GATHER_SLOTS - 1)

    prev = tile_expert_ref[jnp.maximum(i - 1, 0)]
    new_expert = (i == 0) | (tile_expert_ref[i] != prev)

    @pl.when(new_expert)
    def _():
        wg_b[...] = wg_ref[0].astype(bf16)
        wu_b[...] = wu_ref[0].astype(bf16)
        wd_b[...] = wd_ref[0].astype(bf16)

    @pl.when(i < n_used)
    def _():
        wait_gather(slot)
        x = _load_row_tiles(xbuf.at[slot], EXPERT_TILE).astype(bf16)
        gate = _dot(x, wg_b[...])
        up = _dot(x, wu_b[...])
        act = (gate * _sigmoid(gate) * up).astype(bf16)
        _store_row_tiles(ys_ref, _dot(act, wd_b[...]))

    @pl.when(i >= n_used)
    def _():
        ys_ref[...] = jnp.zeros_like(ys_ref)


def _expert_call(tile_expert, n_used, row_token, h1, wg, wu, wd):
    n_tiles = tile_expert.shape[0]
    w_map = lambda i, te, nu, rt: (te[i], 0, 0)
    grid_spec = pltpu.PrefetchScalarGridSpec(
        num_scalar_prefetch=3,
        grid=(n_tiles,),
        in_specs=[pl.BlockSpec(memory_space=pl.ANY),
                  pl.BlockSpec((1, D_MODEL, EXPERT_FF), w_map),
                  pl.BlockSpec((1, D_MODEL, EXPERT_FF), w_map),
                  pl.BlockSpec((1, EXPERT_FF, D_MODEL), w_map)],
        out_specs=pl.BlockSpec((EXPERT_TILE * ROW_TILE_ROWS, LANES), lambda i, te, nu, rt: (i, 0)),
        scratch_shapes=[pltpu.VMEM((GATHER_SLOTS, EXPERT_TILE * ROW_TILE_ROWS, LANES), jnp.float32),
                        pltpu.SemaphoreType.DMA((GATHER_SLOTS,)),
                        pltpu.VMEM((D_MODEL, EXPERT_FF), jnp.bfloat16),
                        pltpu.VMEM((D_MODEL, EXPERT_FF), jnp.bfloat16),
                        pltpu.VMEM((EXPERT_FF, D_MODEL), jnp.bfloat16)],
    )
    return pl.pallas_call(
        _expert_kernel,
        grid_spec=grid_spec,
        out_shape=jax.ShapeDtypeStruct((n_tiles * EXPERT_TILE * ROW_TILE_ROWS, LANES), jnp.float32),
        compiler_params=pltpu.CompilerParams(dimension_semantics=("arbitrary",),
                                             vmem_limit_bytes=VMEM_LIMIT),
        name="experts",
    )(tile_expert, n_used, row_token, h1, wg, wu, wd)


def _combine_kernel(pos_ref, h1_ref, route_ref, pp_ref, ps_ref, ys_hbm, w_ple_proj_ref, w_ple_gate_ref,
                    ln2_g_ref, ln2_b_ref, yp_ref, ys_out_ref, ybuf, sems, *, n_prompt_tiles):
    i = pl.program_id(0)
    n_steps = pl.num_programs(0)
    bf16 = jnp.bfloat16
    slot = lax.rem(i, 2)

    def start_gather(tile, dst_slot):
        def issue(r, carry):
            pair = 2 * (tile * TOKEN_TILE + r)
            dst = pl.multiple_of(r * ROW_TILE_ROWS, ROW_TILE_ROWS)
            for which in range(2):
                src = pl.multiple_of(pos_ref[pair + which] * ROW_TILE_ROWS, ROW_TILE_ROWS)
                pltpu.make_async_copy(ys_hbm.at[pl.ds(src, ROW_TILE_ROWS)],
                                      ybuf.at[dst_slot, which, pl.ds(dst, ROW_TILE_ROWS)],
                                      sems.at[dst_slot]).start(priority=which)
            return carry
        lax.fori_loop(0, TOKEN_TILE, issue, 0, unroll=4)

    def wait_gather(dst_slot):
        for which in range(2):
            pltpu.make_async_copy(ys_hbm.at[pl.ds(0, TOKEN_TILE * ROW_TILE_ROWS)], ybuf.at[dst_slot, which],
                                  sems.at[dst_slot]).wait()

    @pl.when(i == 0)
    def _():
        start_gather(0, 0)

    @pl.when(i + 1 < n_steps)
    def _():
        start_gather(i + 1, 1 - slot)

    h1 = _load_row_tiles(h1_ref, TOKEN_TILE)
    is_prompt = i < n_prompt_tiles
    p = jnp.where(is_prompt, pp_ref[...], ps_ref[...])
    pe = _dot(p.astype(bf16), w_ple_proj_ref[...]) * _sigmoid(_dot(h1.astype(bf16), w_ple_gate_ref[...]))
    route = route_ref[...]
    w1 = route[:, 2:3]
    w2 = route[:, 3:4]
    wait_gather(slot)
    y1 = _load_row_tiles(ybuf.at[slot, 0], TOKEN_TILE)
    y2 = _load_row_tiles(ybuf.at[slot, 1], TOKEN_TILE)
    total = DN_ALPHA * h1 + pe + w1 * y1 + w2 * y2
    out = _layer_norm(total, ln2_g_ref[...], ln2_b_ref[...])

    @pl.when(is_prompt)
    def _():
        yp_ref[...] = out

    @pl.when(jnp.logical_not(is_prompt))
    def _():
        ys_out_ref[...] = out


def _combine_call(pos, h1, route, p_prompt, p_sample, ys, w_ple_proj, w_ple_gate, ln2_g, ln2_b):
    n_tok = route.shape[0]
    n_prompt_tiles = p_prompt.shape[0] // TOKEN_TILE
    n_sample_tiles = p_sample.shape[0] // TOKEN_TILE
    n_steps = n_tok // TOKEN_TILE
    assert n_steps == n_prompt_tiles + n_sample_tiles
    tok_spec = lambda cols: pl.BlockSpec((TOKEN_TILE, cols), lambda i, pos: (i, 0))
    prompt_spec = lambda cols: pl.BlockSpec((TOKEN_TILE, cols),
                                            lambda i, pos: (jnp.minimum(i, n_prompt_tiles - 1), 0))
    sample_spec = lambda cols: pl.BlockSpec((TOKEN_TILE, cols),
                                            lambda i, pos: (jnp.maximum(i - n_prompt_tiles, 0), 0))
    const_spec = lambda shape: pl.BlockSpec(shape, lambda i, pos: (0, 0))
    grid_spec = pltpu.PrefetchScalarGridSpec(
        num_scalar_prefetch=1,
        grid=(n_steps,),
        in_specs=[pl.BlockSpec((TOKEN_TILE * ROW_TILE_ROWS, LANES), lambda i, pos: (i, 0)), tok_spec(LANES),
                  prompt_spec(PLE_DIM), sample_spec(PLE_DIM),
                  pl.BlockSpec(memory_space=pl.ANY), const_spec(w_ple_proj.shape), const_spec(w_ple_gate.shape),
                  const_spec(ln2_g.shape), const_spec(ln2_b.shape)],
        out_specs=(prompt_spec(D_MODEL), sample_spec(D_MODEL)),
        scratch_shapes=[pltpu.VMEM((2, 2, TOKEN_TILE * ROW_TILE_ROWS, LANES), jnp.float32),
                        pltpu.SemaphoreType.DMA((2,))],
    )
    return pl.pallas_call(
        functools.partial(_combine_kernel, n_prompt_tiles=n_prompt_tiles),
        grid_spec=grid_spec,
        out_shape=(jax.ShapeDtypeStruct((n_prompt_tiles * TOKEN_TILE, D_MODEL), jnp.float32),
                   jax.ShapeDtypeStruct((n_sample_tiles * TOKEN_TILE, D_MODEL), jnp.float32)),
        compiler_params=pltpu.CompilerParams(dimension_semantics=("arbitrary",),
                                             vmem_limit_bytes=VMEM_LIMIT),
        name="combine",
    )(pos, h1, route, p_prompt, p_sample, ys, w_ple_proj, w_ple_gate, ln2_g, ln2_b)


def _dispatch_plan(route, n_tiles):
    experts = route[:, :2].astype(jnp.int32).reshape(-1)
    n_pairs = experts.shape[0]
    expert_ids = jnp.arange(N_EXPERTS, dtype=jnp.int32)
    onehot = (experts[None, :] == expert_ids[:, None]).astype(jnp.int32)
    running = jnp.cumsum(onehot, axis=1)
    counts = running[:, -1]
    tiles_per = (counts + EXPERT_TILE - 1) // EXPERT_TILE
    tile_end = jnp.cumsum(tiles_per)
    row_start = (tile_end - tiles_per) * EXPERT_TILE
    pos = jnp.sum(onehot * (running - 1 + row_start[:, None]), axis=0)
    n_used = tile_end[-1]
    tile_index = jnp.arange(n_tiles, dtype=jnp.int32)
    tile_expert = jnp.sum((jnp.minimum(tile_index, n_used - 1)[:, None] >= tile_end[None, :]).astype(jnp.int32),
                          axis=1)

    order = jnp.argsort(experts, stable=True).astype(jnp.int32)
    pair_start = jnp.cumsum(counts) - counts
    k = (tile_index * EXPERT_TILE - row_start[tile_expert])[:, None] + jnp.arange(EXPERT_TILE, dtype=jnp.int32)
    valid = k < counts[tile_expert][:, None]
    src_pair = order[jnp.clip(pair_start[tile_expert][:, None] + k, 0, n_pairs - 1)]
    row_token = jnp.where(valid, src_pair // 2, 0).reshape(-1)
    return (pos.astype(jnp.int32), row_token.astype(jnp.int32), tile_expert.astype(jnp.int32),
            n_used.reshape(1).astype(jnp.int32))


def kernel(x_prompt, x_sample, p_prompt, p_sample, state_ret, state_conv, ln_in_g, ln_in_b, w_in, w_ret_out,
           conv_w, conv_b, conv_ln_g, conv_ln_b, w_conv_out, w_out, ln1_g, ln1_b, w_route_g, b_route_g,
           w_route_e, b_route_e, w_exp_gate, w_exp_up, w_exp_down, w_ple_proj, w_ple_gate, ln2_g, ln2_b):
    assert w_in.shape[0] == DEPTH == 1
    bf16 = jnp.bfloat16
    batch, seq, _ = x_prompt.shape
    dec_batch, dec_seq, _ = x_sample.shape
    assert dec_seq == CHUNK and seq % TOKEN_TILE == 0 and TOKEN_TILE % CHUNK == 0
    row = lambda a: a.reshape(1, -1)
    n_prompt = batch * seq
    n_sample = dec_batch * dec_seq
    n_tok = n_prompt + n_sample

    w_route = jnp.concatenate(
        [w_route_g[0], jnp.transpose(w_route_e[0], (1, 0, 2)).reshape(D_MODEL, N_EXPERTS),
         jnp.zeros((D_MODEL, LANES - ROUTE_COLS), jnp.float32)], axis=1)
    w_route_hi = w_route.astype(bf16)
    w_route_lo = (w_route - w_route_hi.astype(jnp.float32)).astype(bf16)
    b_route = jnp.concatenate([b_route_g[0], b_route_e[0].reshape(-1),
                               jnp.zeros((LANES - ROUTE_COLS,), jnp.float32)]).reshape(1, LANES)
    conv_w_pad = jnp.concatenate([conv_w[0], jnp.zeros((HIST_ROWS - CONV_WIDTH, D_MODEL), jnp.float32)], axis=0)
    weights = (row(ln_in_g), row(ln_in_b), w_in[0].astype(bf16), w_ret_out[0].astype(bf16), conv_w_pad,
               row(conv_b[0]), row(conv_ln_g[0]), row(conv_ln_b[0]), w_conv_out[0].astype(bf16),
               w_out[0].astype(bf16), row(ln1_g[0]), row(ln1_b[0]),
               jnp.concatenate([w_route_hi, w_route_lo], axis=1), w_route_hi, b_route)

    steps_p = seq // TOKEN_TILE
    sample_tiles = n_sample // TOKEN_TILE
    assert sample_tiles % steps_p == 0
    tables_p = _mix_tables(np.arange(seq), 1, TOKEN_TILE)
    rinit_p = jnp.zeros((batch, RET_HEADS, RET_DK, RET_DV), jnp.float32)
    cinit_p = jnp.zeros((batch, HIST_ROWS, D_MODEL), jnp.float32)
    h1, route, rnew_p, cnew_p = _mix_call(
        x_prompt.reshape(n_prompt, D_MODEL), rinit_p, cinit_p, tables_p, weights, None, n_tok_total=n_tok,
        n_seq=1, seq_rows=TOKEN_TILE, n_groups=batch, n_steps=steps_p, table_per_step=True,
        tile_offset=0, n_fill_groups=sample_tiles // steps_p)

    seqs_per_tile = TOKEN_TILE // dec_seq
    assert dec_batch % seqs_per_tile == 0
    pos_s = np.tile(PAST_LEN + np.arange(dec_seq), seqs_per_tile)
    tables_s = _mix_tables(pos_s, seqs_per_tile, dec_seq)
    cinit_s = jnp.pad(state_conv[0], ((0, 0), (HIST_PAD, 0), (0, 0)))
    h1, route, rnew_s, cnew_s = _mix_call(
        x_sample.reshape(n_sample, D_MODEL), state_ret[0], cinit_s, tables_s, weights, (h1, route),
        n_tok_total=n_tok, n_seq=seqs_per_tile, seq_rows=dec_seq, n_groups=dec_batch // seqs_per_tile,
        n_steps=1, table_per_step=False, tile_offset=n_prompt // TOKEN_TILE, n_fill_groups=0)

    n_tiles = (2 * n_tok) // EXPERT_TILE + N_EXPERTS
    pos, row_token, tile_expert, n_used = _dispatch_plan(route, n_tiles)
    ys = _expert_call(tile_expert, n_used, row_token, h1,
                      w_exp_gate[0].reshape(N_EXPERTS, D_MODEL, EXPERT_FF),
                      w_exp_up[0].reshape(N_EXPERTS, D_MODEL, EXPERT_FF),
                      w_exp_down[0].reshape(N_EXPERTS, EXPERT_FF, D_MODEL))
    y_p, y_s = _combine_call(pos, h1, route, p_prompt[0].reshape(n_prompt, PLE_DIM),
                             p_sample[0].reshape(n_sample, PLE_DIM), ys, w_ple_proj[0].astype(bf16),
                             w_ple_gate[0].astype(bf16), row(ln2_g[0]), row(ln2_b[0]))

    return (y_p.reshape(batch, seq, D_MODEL), y_s.reshape(dec_batch, dec_seq, D_MODEL), rnew_p[None],
            cnew_p[None, :, HIST_PAD:, :], rnew_s[None], cnew_s[None, :, HIST_PAD:, :])
```

```python
import functools
import math

import jax
import jax.numpy as jnp
import numpy as np
from jax import lax
from jax.experimental import pallas as pl
from jax.experimental.pallas import tpu as pltpu

D_MODEL = 1024
RET_HEADS = 4
RET_DK = 128
RET_DV = 256
RET_QK = RET_HEADS * RET_DK
RET_V = RET_HEADS * RET_DV
CHUNK = 64
CONV_WIDTH = 31
CONV_HIST = CONV_WIDTH - 1
HIST_ROWS = 32
HIST_PAD = HIST_ROWS - CONV_HIST
N_GROUPS = 4
EXPERTS_PER_GROUP = 8
N_EXPERTS = N_GROUPS * EXPERTS_PER_GROUP
EXPERT_FF = 512
PLE_DIM = 256
PAST_LEN = 4096
LN_EPS = 1e-5
ROPE_BASE = 10000.0
DEPTH = 1
DN_ALPHA = float((2 * DEPTH) ** 0.25)

LANES = 128
TOKEN_TILE = 256
EXPERT_TILE = 256
GATHER_SLOTS = 3
LANE_CHUNKS = D_MODEL // LANES
ROW_TILE_ROWS = D_MODEL // LANES
CONV_ROW_BLOCK = 64
VMEM_LIMIT = 56 * 1024 * 1024

C_Q = 0
C_K = RET_QK
C_V = 2 * RET_QK
C_G = C_V + RET_V
C_GLU_A = C_G + RET_V
C_GLU_B = C_GLU_A + D_MODEL
C_GA = C_GLU_B + D_MODEL
C_GB = C_GA + D_MODEL

ROUTE_COLS = N_GROUPS + N_EXPERTS
PAIR_BITS = 16
NEG_BIG = -1e30
NEG_LOG2_E = -math.log2(math.e)


def _sigmoid(x):
    return 1.0 / (1.0 + jnp.exp2(x * NEG_LOG2_E))


def _normalize(x):
    mu = jnp.mean(x, axis=-1, keepdims=True)
    xc = x - mu
    var = jnp.mean(xc * xc, axis=-1, keepdims=True)
    return xc * lax.rsqrt(var + LN_EPS)


def _layer_norm(x, g, b):
    return _normalize(x) * g + b


def _dot(a, b):
    return jnp.dot(a, b, preferred_element_type=jnp.float32)


def _store_row_tiles(ref, x):
    n = x.shape[0]
    for j in range(ROW_TILE_ROWS):
        ref[pl.ds(j, n, stride=ROW_TILE_ROWS), :] = x[:, j * LANES:(j + 1) * LANES]


def _load_row_tiles(ref, n):
    return jnp.concatenate([ref[pl.ds(j, n, stride=ROW_TILE_ROWS), :] for j in range(ROW_TILE_ROWS)], axis=1)


N_MIX_INPUTS = 25


def _mix_kernel(*refs, n_real_groups, n_fill_groups, aliased, **tile_params):
    if aliased:
        refs = refs[:N_MIX_INPUTS] + refs[N_MIX_INPUTS + 2:]
    h1_ref, route_ref = refs[N_MIX_INPUTS], refs[N_MIX_INPUTS + 1]
    if n_fill_groups == 0:
        _mix_tile(*refs, **tile_params)
        return
    group = pl.program_id(0)

    @pl.when(group < n_real_groups)
    def _():
        _mix_tile(*refs, **tile_params)

    @pl.when(group >= n_real_groups)
    def _():
        h1_ref[...] = jnp.zeros_like(h1_ref)
        route_ref[...] = jnp.zeros_like(route_ref)


def _mix_tile(x_ref, rinit_ref, cinit_ref, cq_ref, sq_ref, ck_ref, sk_ref,
              xi_ref, zeta_ref, mask_ref,
              ln_in_g_ref, ln_in_b_ref, w_in_ref, w_ret_out_ref, conv_w_ref, conv_b_ref,
              conv_ln_g_ref, conv_ln_b_ref, w_conv_out_ref, w_out_ref, ln1_g_ref, ln1_b_ref,
              w_route_ref, w_route_hi_ref, b_route_ref,
              h1_ref, route_ref, rnew_ref, cnew_ref,
              r_scr, ubuf, c_scr,
              *, n_seq, seq_rows, state_decay):
    t = pl.program_id(1)
    last_t = pl.num_programs(1) - 1
    bf16 = jnp.bfloat16

    @pl.when(t == 0)
    def _():
        r_scr[...] = rinit_ref[...]
        for j in range(LANE_CHUNKS):
            ubuf[:, j, 0:HIST_ROWS, :] = cinit_ref[:, :, j * LANES:(j + 1) * LANES]

    h = _layer_norm(x_ref[...], ln_in_g_ref[...], ln_in_b_ref[...])
    hb = h.astype(bf16)

    u = _dot(hb, w_in_ref[:, C_GLU_A:C_GLU_A + D_MODEL]) * _sigmoid(
        _dot(hb, w_in_ref[:, C_GLU_B:C_GLU_B + D_MODEL]))
    for sq_i in range(n_seq):
        for j in range(LANE_CHUNKS):
            ubuf[sq_i, j, HIST_ROWS:HIST_ROWS + seq_rows, :] = u[sq_i * seq_rows:(sq_i + 1) * seq_rows,
                                                                 j * LANES:(j + 1) * LANES]

    row_block = min(CONV_ROW_BLOCK, seq_rows)

    def conv_lane_chunk(j):
        lane = slice(j * LANES, (j + 1) * LANES)
        for sq_i in range(n_seq):
            for rb in range(seq_rows // row_block):
                r0 = rb * row_block + HIST_PAD
                acc = None
                for kk in range(CONV_WIDTH):
                    term = conv_w_ref[kk:kk + 1, lane] * ubuf[sq_i, j, r0 + kk:r0 + kk + row_block, :]
                    acc = term if acc is None else acc + term
                out_row = sq_i * seq_rows + rb * row_block
                c_scr[out_row:out_row + row_block, lane] = acc

    assert LANE_CHUNKS == 2 * RET_HEADS

    q = _dot(hb, w_in_ref[:, C_Q:C_Q + RET_QK])
    conv_lane_chunk(0)
    k = _dot(hb, w_in_ref[:, C_K:C_K + RET_QK])
    conv_lane_chunk(1)
    v = _dot(hb, w_in_ref[:, C_V:C_V + RET_V])
    conv_lane_chunk(2)
    g = _dot(hb, w_in_ref[:, C_G:C_G + RET_V])
    g = g * _sigmoid(g)
    ga = _sigmoid(_dot(hb, w_in_ref[:, C_GA:C_GA + D_MODEL]))
    conv_lane_chunk(3)
    gb = _sigmoid(_dot(hb, w_in_ref[:, C_GB:C_GB + D_MODEL]))
    cq, sq, ck, sk = cq_ref[...], sq_ref[...], ck_ref[...], sk_ref[...]

    y_ret = None
    for hd in range(RET_HEADS):
        qh = q[:, hd * RET_DK:(hd + 1) * RET_DK]
        kh = k[:, hd * RET_DK:(hd + 1) * RET_DK]
        qr = qh * cq + pltpu.roll(qh, RET_DK // 2, 1) * sq
        kr = kh * ck + pltpu.roll(kh, RET_DK // 2, 1) * sk
        vb = v[:, hd * RET_DV:(hd + 1) * RET_DV].astype(bf16)
        s = lax.dot_general(qr.astype(bf16), kr.astype(bf16), (((1,), (1,)), ((), ())),
                            preferred_element_type=jnp.float32)
        inner = _dot((s * mask_ref[hd]).astype(bf16), vb)
        qx = (qr * xi_ref[hd]).astype(bf16)
        kz = (kr * zeta_ref[hd]).astype(bf16)
        o_parts = []
        for sq_i in range(n_seq):
            rows = slice(sq_i * seq_rows, (sq_i + 1) * seq_rows)
            r_old = r_scr[sq_i, hd]
            o_parts.append(inner[rows] + _dot(qx[rows], r_old.astype(bf16)))
            kv = lax.dot_general(kz[rows], vb[rows], (((0,), (0,)), ((), ())),
                                 preferred_element_type=jnp.float32)
            r_scr[sq_i, hd] = state_decay[hd] * r_old + kv
        o = o_parts[0] if n_seq == 1 else jnp.concatenate(o_parts, axis=0)
        on = _normalize(o)
        gated = (on * g[:, hd * RET_DV:(hd + 1) * RET_DV]).astype(bf16)
        part = _dot(gated, w_ret_out_ref[hd * RET_DV:(hd + 1) * RET_DV, :])
        y_ret = part if y_ret is None else y_ret + part
        conv_lane_chunk(RET_HEADS + hd)

    for sq_i in range(n_seq):
        for j in range(LANE_CHUNKS):
            ubuf[sq_i, j, 0:HIST_ROWS, :] = ubuf[sq_i, j, seq_rows:seq_rows + HIST_ROWS, :]

    c = _layer_norm(c_scr[...] + conv_b_ref[...], conv_ln_g_ref[...], conv_ln_b_ref[...])
    c = c * _sigmoid(c)
    y_conv = _dot(c.astype(bf16), w_conv_out_ref[...])

    merged =(ga * y_ret + gb * y_conv).astype(bf16)
    h1 = _layer_norm(DN_ALPHA * h + _dot(merged, w_out_ref[...]), ln1_g_ref[...], ln1_b_ref[...])
    _store_row_tiles(h1_ref, h1)

    h1_hi = h1.astype(bf16)
    h1_lo = (h1 - h1_hi.astype(jnp.float32)).astype(bf16)
    both = _dot(h1_hi, w_route_ref[...])
    logits = both[:, :LANES] + both[:, LANES:] + _dot(h1_lo, w_route_hi_ref[...]) + b_route_ref[...]
    lane = lax.broadcasted_iota(jnp.int32, logits.shape, 1)
    lane_f = lane.astype(jnp.float32)
    is_group = lane < N_GROUPS
    gl = jnp.where(is_group, logits, NEG_BIG)
    gmax = jnp.max(gl, axis=-1, keepdims=True)
    gexp = jnp.where(is_group, jnp.exp(gl - gmax), 0.0)
    gprob = gexp / jnp.sum(gexp, axis=-1, keepdims=True)
    gp = jnp.max(gprob, axis=-1, keepdims=True)
    gi = jnp.min(jnp.where(is_group & (gprob == gp), lane_f, float(LANES)), axis=-1, keepdims=True)
    e_lo = N_GROUPS + EXPERTS_PER_GROUP * gi
    in_group = (lane_f >= e_lo) & (lane_f < e_lo + EXPERTS_PER_GROUP)
    el = jnp.where(in_group, logits, NEG_BIG)
    ev1 = jnp.max(el, axis=-1, keepdims=True)
    ei1 = jnp.min(jnp.where(in_group & (el == ev1), lane_f, float(LANES)), axis=-1, keepdims=True)
    rest = in_group & (lane_f != ei1)
    el2 = jnp.where(rest, logits, NEG_BIG)
    ev2 = jnp.max(el2, axis=-1, keepdims=True)
    ei2 = jnp.min(jnp.where(rest & (el2 == ev2), lane_f, float(LANES)), axis=-1, keepdims=True)
    x2 = jnp.exp(ev2 - ev1)
    w1 = gp / (1.0 + x2)
    w2 = w1 * x2
    route = jnp.where(lane == 0, ei1 - N_GROUPS, 0.0)
    route = jnp.where(lane == 1, ei2 - N_GROUPS, route)
    route = jnp.where(lane == 2, w1, route)
    route = jnp.where(lane == 3, w2, route)
    route_ref[...] = route

    @pl.when(t == last_t)
    def _():
        rnew_ref[...] = r_scr[...]
        for j in range(LANE_CHUNKS):
            cnew_ref[:, :, j * LANES:(j + 1) * LANES] = ubuf[:, j, 0:HIST_ROWS, :]


def _const_spec(shape):
    zeros = (0,) * len(shape)
    return pl.BlockSpec(shape, lambda b, t: zeros, pipeline_mode=pl.Buffered(1))


def _mix_call(x, rinit, cinit, tables, weights, shared, *, n_tok_total, n_seq, seq_rows, n_groups, n_steps,
              table_per_step, tile_offset, n_fill_groups):
    tt = n_seq * seq_rows
    assert tt == TOKEN_TILE and x.shape[0] == n_groups * n_steps * tt
    log_decay = [math.log1p(-(2.0 ** (-5.0 - hd))) for hd in range(RET_HEADS)]
    state_decay = tuple(math.exp(lg * seq_rows) for lg in log_decay)
    cq, sq, ck, sk, xi, zeta, mask = tables
    last_group = n_groups - 1
    own = lambda b: jnp.minimum(b, last_group)

    in_tok_spec = pl.BlockSpec((tt, D_MODEL), lambda b, t: (own(b) * n_steps + t, 0))
    out_tok_spec = lambda rows: pl.BlockSpec((rows, LANES), lambda b, t: (tile_offset + b * n_steps + t, 0))
    rope_spec = pl.BlockSpec((tt, LANES), (lambda b, t: (t, 0)) if table_per_step else (lambda b, t: (0, 0)))
    state_r_spec = pl.BlockSpec((n_seq, RET_HEADS, RET_DK, RET_DV), lambda b, t: (own(b), 0, 0, 0))
    state_c_spec = pl.BlockSpec((n_seq, HIST_ROWS, D_MODEL), lambda b, t: (own(b), 0, 0))
    any_spec = pl.BlockSpec(memory_space=pl.ANY)

    in_specs = [in_tok_spec, state_r_spec, state_c_spec, rope_spec, rope_spec, rope_spec, rope_spec,
                _const_spec(xi.shape), _const_spec(zeta.shape), _const_spec(mask.shape)]
    in_specs += [_const_spec(w.shape) for w in weights]
    assert len(in_specs) == N_MIX_INPUTS
    aliases = {}
    extra = ()
    if shared is not None:
        assert n_fill_groups == 0
        in_specs += [any_spec, any_spec]
        aliases = {N_MIX_INPUTS: 0, N_MIX_INPUTS + 1: 1}
        extra = tuple(shared)
    out_shape = (jax.ShapeDtypeStruct((n_tok_total * ROW_TILE_ROWS, LANES), jnp.float32),
                 jax.ShapeDtypeStruct((n_tok_total, LANES), jnp.float32),
                 jax.ShapeDtypeStruct(rinit.shape, jnp.float32),
                 jax.ShapeDtypeStruct(cinit.shape, jnp.float32))
    out_specs = (out_tok_spec(tt * ROW_TILE_ROWS), out_tok_spec(tt), state_r_spec, state_c_spec)
    body = functools.partial(_mix_kernel, n_real_groups=n_groups, n_fill_groups=n_fill_groups,
                             aliased=shared is not None,
                             n_seq=n_seq, seq_rows=seq_rows, state_decay=state_decay)
    return pl.pallas_call(
        body,
        grid=(n_groups + n_fill_groups, n_steps),
        in_specs=in_specs,
        out_specs=out_specs,
        out_shape=out_shape,
        input_output_aliases=aliases,
        scratch_shapes=[pltpu.VMEM((n_seq, RET_HEADS, RET_DK, RET_DV), jnp.float32),
                        pltpu.VMEM((n_seq, LANE_CHUNKS, HIST_ROWS + seq_rows, LANES), jnp.float32),
                        pltpu.VMEM((tt, D_MODEL), jnp.float32)],
        compiler_params=pltpu.CompilerParams(dimension_semantics=("arbitrary", "arbitrary"),
                                             vmem_limit_bytes=VMEM_LIMIT),
        name="mix",
    )(x, rinit, cinit, cq, sq, ck, sk, xi, zeta, mask, *weights, *extra)


def _mix_tables(positions, n_seq, seq_rows):
    half = RET_DK // 2
    inv = ROPE_BASE ** (-np.arange(half, dtype=np.float64) / half)
    ang = np.asarray(positions, np.float64)[:, None] * inv[None, :]
    cos, sin = np.cos(ang), np.sin(ang)
    cq = np.concatenate([cos, cos], axis=1)
    sq = np.concatenate([-sin, sin], axis=1)
    scale = RET_DK ** -0.5
    lg = np.log1p(-np.exp2(-5.0 - np.arange(RET_HEADS, dtype=np.float64)))
    tt = n_seq * seq_rows
    i = np.arange(tt)
    loc = (i % seq_rows).astype(np.float64)
    xi = np.exp(lg[:, None] * (loc + 1.0)[None, :])
    zeta = np.exp(lg[:, None] * (seq_rows - 1.0 - loc)[None, :])
    xi = np.broadcast_to(xi[:, :, None], (RET_HEADS, tt, LANES))
    zeta = np.broadcast_to(zeta[:, :, None], (RET_HEADS, tt, LANES))
    same_seq = (i[:, None] // seq_rows) == (i[None, :] // seq_rows)
    visible = same_seq & ((i[None, :] // CHUNK) <= (i[:, None] // CHUNK))
    dist = np.abs(i[:, None] - i[None, :]).astype(np.float64)
    mask = np.where(visible[None], np.exp(lg[:, None, None] * dist[None]), 0.0)
    tables = (cq, sq, cq * scale, sq * scale, xi, zeta, mask)
    return tuple(jnp.asarray(np.ascontiguousarray(tab), jnp.float32) for tab in tables)


def _expert_kernel(tile_expert_ref, n_used_ref, tile_pair_base_ref, sorted_token_ref, h1_hbm, wg_ref, wu_ref,
                   wd_ref, ys_ref, xbuf, sems, wg_b, wu_b, wd_b):
    i = pl.program_id(0)
    n_used = n_used_ref[0]
    bf16 = jnp.bfloat16
    slot = lax.rem(i, GATHER_SLOTS)

    def start_gather(tile):
        dst_slot = lax.rem(tile, GATHER_SLOTS)
        pair_base = tile_pair_base_ref[tile]

        def issue(r, carry):
            src = pl.multiple_of(sorted_token_ref[pair_base + r] * ROW_TILE_ROWS, ROW_TILE_ROWS)
            dst = pl.multiple_of(r * ROW_TILE_ROWS, ROW_TILE_ROWS)
            pltpu.make_async_copy(h1_hbm.at[pl.ds(src, ROW_TILE_ROWS)],
                                  xbuf.at[dst_slot, pl.ds(dst, ROW_TILE_ROWS)],
                                  sems.at[dst_slot]).start(priority=1)
            return carry
        lax.fori_loop(0, EXPERT_TILE, issue, 0, unroll=8)

    def wait_gather(dst_slot):
        pltpu.make_async_copy(h1_hbm.at[pl.ds(0, EXPERT_TILE * ROW_TILE_ROWS)], xbuf.at[dst_slot],
                              sems.at[dst_slot]).wait()

    for first in range(GATHER_SLOTS - 1):
        @pl.when((i == 0) & (first < n_used))
        def _(first=first):
            start_gather(first)

    @pl.when(i + GATHER_SLOTS - 1 < n_used)
    def _():
        start_gather(i + GATHER_SLOTS - 1)

    prev = tile_expert_ref[jnp.maximum(i - 1, 0)]
    new_expert = (i == 0) | (tile_expert_ref[i] != prev)

    @pl.when(new_expert)
    def _():
        wg_b[...] = wg_ref[0].astype(bf16)
        wu_b[...] = wu_ref[0].astype(bf16)
        wd_b[...] = wd_ref[0].astype(bf16)

    @pl.when(i < n_used)
    def _():
        wait_gather(slot)
        x = _load_row_tiles(xbuf.at[slot], EXPERT_TILE).astype(bf16)
        gate = _dot(x, wg_b[...])
        up = _dot(x, wu_b[...])
        act = (gate * _sigmoid(gate) * up).astype(bf16)
        _store_row_tiles(ys_ref, _dot(act, wd_b[...]))

    @pl.when(i >= n_used)
    def _():
        ys_ref[...] = jnp.zeros_like(ys_ref)


def _expert_call(tile_expert, n_used, tile_pair_base, sorted_token, h1, wg, wu, wd):
    n_tiles = tile_expert.shape[0]
    w_map = lambda i, te, nu, pb, st: (te[i], 0, 0)
    grid_spec = pltpu.PrefetchScalarGridSpec(
        num_scalar_prefetch=4,
        grid=(n_tiles,),
        in_specs=[pl.BlockSpec(memory_space=pl.ANY),
                  pl.BlockSpec((1, D_MODEL, EXPERT_FF), w_map),
                  pl.BlockSpec((1, D_MODEL, EXPERT_FF), w_map),
                  pl.BlockSpec((1, EXPERT_FF, D_MODEL), w_map)],
        out_specs=pl.BlockSpec((EXPERT_TILE * ROW_TILE_ROWS, LANES), lambda i, te, nu, pb, st: (i, 0)),
        scratch_shapes=[pltpu.VMEM((GATHER_SLOTS, EXPERT_TILE * ROW_TILE_ROWS, LANES), jnp.float32),
                        pltpu.SemaphoreType.DMA((GATHER_SLOTS,)),
                        pltpu.VMEM((D_MODEL, EXPERT_FF), jnp.bfloat16),
                        pltpu.VMEM((D_MODEL, EXPERT_FF), jnp.bfloat16),
                        pltpu.VMEM((EXPERT_FF, D_MODEL), jnp.bfloat16)],
    )
    return pl.pallas_call(
        _expert_kernel,
        grid_spec=grid_spec,
        out_shape=jax.ShapeDtypeStruct((n_tiles * EXPERT_TILE * ROW_TILE_ROWS, LANES), jnp.float32),
        compiler_params=pltpu.CompilerParams(dimension_semantics=("arbitrary",),
                                             vmem_limit_bytes=VMEM_LIMIT),
        name="experts",
    )(tile_expert, n_used, tile_pair_base, sorted_token, h1, wg, wu, wd)


def _combine_kernel(pos_ref, h1_ref, route_ref, pp_ref, ps_ref, ys_hbm, w_ple_proj_ref, w_ple_gate_ref,
                    ln2_g_ref, ln2_b_ref, yp_ref, ys_out_ref, ybuf, sems, *, n_prompt_tiles):
    i = pl.program_id(0)
    n_steps = pl.num_programs(0)
    bf16 = jnp.bfloat16
    slot = lax.rem(i, 2)

    def start_gather(tile, dst_slot):
        def issue(r, carry):
            pair = 2 * (tile * TOKEN_TILE + r)
            dst = pl.multiple_of(r * ROW_TILE_ROWS, ROW_TILE_ROWS)
            for which in range(2):
                src = pl.multiple_of(pos_ref[pair + which] * ROW_TILE_ROWS, ROW_TILE_ROWS)
                pltpu.make_async_copy(ys_hbm.at[pl.ds(src, ROW_TILE_ROWS)],
                                      ybuf.at[dst_slot, which, pl.ds(dst, ROW_TILE_ROWS)],
                                      sems.at[dst_slot]).start(priority=which)
            return carry
        lax.fori_loop(0, TOKEN_TILE, issue, 0, unroll=4)

    def wait_gather(dst_slot):
        for which in range(2):
            pltpu.make_async_copy(ys_hbm.at[pl.ds(0, TOKEN_TILE * ROW_TILE_ROWS)], ybuf.at[dst_slot, which],
                                  sems.at[dst_slot]).wait()

    @pl.when(i == 0)
    def _():
        start_gather(0, 0)

    @pl.when(i + 1 < n_steps)
    def _():
        start_gather(i + 1, 1 - slot)

    h1 = _load_row_tiles(h1_ref, TOKEN_TILE)
    is_prompt = i < n_prompt_tiles
    p = jnp.where(is_prompt, pp_ref[...], ps_ref[...])
    pe = _dot(p.astype(bf16), w_ple_proj_ref[...]) * _sigmoid(_dot(h1.astype(bf16), w_ple_gate_ref[...]))
    route = route_ref[...]
    w1 = route[:, 2:3]
    w2 = route[:, 3:4]
    wait_gather(slot)
    y1 = _load_row_tiles(ybuf.at[slot, 0], TOKEN_TILE)
    y2 = _load_row_tiles(ybuf.at[slot, 1], TOKEN_TILE)
    total = DN_ALPHA * h1 + pe + w1 * y1 + w2 * y2
    out = _layer_norm(total, ln2_g_ref[...], ln2_b_ref[...])

    @pl.when(is_prompt)
    def _():
        yp_ref[...] = out

    @pl.when(jnp.logical_not(is_prompt))
    def _():
        ys_out_ref[...] = out


def _combine_call(pos, h1, route, p_prompt, p_sample, ys, w_ple_proj, w_ple_gate, ln2_g, ln2_b):
    n_tok = route.shape[0]
    n_prompt_tiles = p_prompt.shape[0] // TOKEN_TILE
    n_sample_tiles = p_sample.shape[0] // TOKEN_TILE
    n_steps = n_tok // TOKEN_TILE
    assert n_steps == n_prompt_tiles + n_sample_tiles
    tok_spec = lambda cols: pl.BlockSpec((TOKEN_TILE, cols), lambda i, pos: (i, 0))
    prompt_spec = lambda cols: pl.BlockSpec((TOKEN_TILE, cols),
                                            lambda i, pos: (jnp.minimum(i, n_prompt_tiles - 1), 0))
    sample_spec = lambda cols: pl.BlockSpec((TOKEN_TILE, cols),
                                            lambda i, pos: (jnp.maximum(i - n_prompt_tiles, 0), 0))
    const_spec = lambda shape: pl.BlockSpec(shape, lambda i, pos: (0, 0))
    grid_spec = pltpu.PrefetchScalarGridSpec(
        num_scalar_prefetch=1,
        grid=(n_steps,),
        in_specs=[pl.BlockSpec((TOKEN_TILE * ROW_TILE_ROWS, LANES), lambda i, pos: (i, 0)), tok_spec(LANES),
                  prompt_spec(PLE_DIM), sample_spec(PLE_DIM),
                  pl.BlockSpec(memory_space=pl.ANY), const_spec(w_ple_proj.shape), const_spec(w_ple_gate.shape),
                  const_spec(ln2_g.shape), const_spec(ln2_b.shape)],
        out_specs=(prompt_spec(D_MODEL), sample_spec(D_MODEL)),
        scratch_shapes=[pltpu.VMEM((2, 2, TOKEN_TILE * ROW_TILE_ROWS, LANES), jnp.float32),
                        pltpu.SemaphoreType.DMA((2,))],
    )
    return pl.pallas_call(
        functools.partial(_combine_kernel, n_prompt_tiles=n_prompt_tiles),
        grid_spec=grid_spec,
        out_shape=(jax.ShapeDtypeStruct((n_prompt_tiles * TOKEN_TILE, D_MODEL), jnp.float32),
                   jax.ShapeDtypeStruct((n_sample_tiles * TOKEN_TILE, D_MODEL), jnp.float32)),
        compiler_params=pltpu.CompilerParams(dimension_semantics=("arbitrary",),
                                             vmem_limit_bytes=VMEM_LIMIT),
        name="combine",
    )(pos, h1, route, p_prompt, p_sample, ys, w_ple_proj, w_ple_gate, ln2_g, ln2_b)


def _dispatch_plan(route, n_tiles):
    experts = route[:, :2].astype(jnp.int32).reshape(-1)
    n_pairs = experts.shape[0]
    expert_ids = jnp.arange(N_EXPERTS, dtype=jnp.int32)
    onehot = (experts[None, :] == expert_ids[:, None]).astype(jnp.int32)
    running = jnp.cumsum(onehot, axis=1)
    counts = running[:, -1]
    tiles_per = (counts + EXPERT_TILE - 1) // EXPERT_TILE
    tile_end = jnp.cumsum(tiles_per)
    row_start = (tile_end - tiles_per) * EXPERT_TILE
    pos = jnp.sum(onehot * (running - 1 + row_start[:, None]), axis=0)
    n_used = tile_end[-1]
    tile_index = jnp.arange(n_tiles, dtype=jnp.int32)
    tile_expert = jnp.sum((jnp.minimum(tile_index, n_used - 1)[:, None] >= tile_end[None, :]).astype(jnp.int32),
                          axis=1)

    assert n_pairs <= 1 << PAIR_BITS
    keys = jnp.sort(experts * (1 << PAIR_BITS) + jnp.arange(n_pairs, dtype=jnp.int32))
    sorted_token = jnp.concatenate([(keys & ((1 << PAIR_BITS) - 1)) >> 1,
                                    jnp.zeros((EXPERT_TILE,), jnp.int32)])
    pair_start = jnp.cumsum(counts) - counts
    tile_pair_base = jnp.clip(pair_start[tile_expert] + tile_index * EXPERT_TILE - row_start[tile_expert],
                              0, n_pairs)
    return (pos.astype(jnp.int32), sorted_token, tile_expert.astype(jnp.int32),
            tile_pair_base.astype(jnp.int32), n_used.reshape(1).astype(jnp.int32))


def kernel(x_prompt, x_sample, p_prompt, p_sample, state_ret, state_conv, ln_in_g, ln_in_b, w_in, w_ret_out,
           conv_w, conv_b, conv_ln_g, conv_ln_b, w_conv_out, w_out, ln1_g, ln1_b, w_route_g, b_route_g,
           w_route_e, b_route_e, w_exp_gate, w_exp_up, w_exp_down, w_ple_proj, w_ple_gate, ln2_g, ln2_b):
    assert w_in.shape[0] == DEPTH == 1
    bf16 = jnp.bfloat16
    batch, seq, _ = x_prompt.shape
    dec_batch, dec_seq, _ = x_sample.shape
    assert dec_seq == CHUNK and seq % TOKEN_TILE == 0 and TOKEN_TILE % CHUNK == 0
    row = lambda a: a.reshape(1, -1)
    n_prompt = batch * seq
    n_sample = dec_batch * dec_seq
    n_tok = n_prompt + n_sample

    w_route = jnp.concatenate(
        [w_route_g[0], jnp.transpose(w_route_e[0], (1, 0, 2)).reshape(D_MODEL, N_EXPERTS),
         jnp.zeros((D_MODEL, LANES - ROUTE_COLS), jnp.float32)], axis=1)
    w_route_hi = w_route.astype(bf16)
    w_route_lo = (w_route - w_route_hi.astype(jnp.float32)).astype(bf16)
    b_route = jnp.concatenate([b_route_g[0], b_route_e[0].reshape(-1),
                               jnp.zeros((LANES - ROUTE_COLS,), jnp.float32)]).reshape(1, LANES)
    conv_w_pad = jnp.concatenate([conv_w[0], jnp.zeros((HIST_ROWS - CONV_WIDTH, D_MODEL), jnp.float32)], axis=0)
    weights = (row(ln_in_g), row(ln_in_b), w_in[0].astype(bf16), w_ret_out[0].astype(bf16), conv_w_pad,
               row(conv_b[0]), row(conv_ln_g[0]), row(conv_ln_b[0]), w_conv_out[0].astype(bf16),
               w_out[0].astype(bf16), row(ln1_g[0]), row(ln1_b[0]),
               jnp.concatenate([w_route_hi, w_route_lo], axis=1), w_route_hi, b_route)

    steps_p = seq // TOKEN_TILE
    sample_tiles = n_sample // TOKEN_TILE
    assert sample_tiles % steps_p == 0
    tables_p = _mix_tables(np.arange(seq), 1, TOKEN_TILE)
    rinit_p = jnp.zeros((batch, RET_HEADS, RET_DK, RET_DV), jnp.float32)
    cinit_p = jnp.zeros((batch, HIST_ROWS, D_MODEL), jnp.float32)
    h1, route, rnew_p, cnew_p = _mix_call(
        x_prompt.reshape(n_prompt, D_MODEL), rinit_p, cinit_p, tables_p, weights, None, n_tok_total=n_tok,
        n_seq=1, seq_rows=TOKEN_TILE, n_groups=batch, n_steps=steps_p, table_per_step=True,
        tile_offset=0, n_fill_groups=sample_tiles // steps_p)

    seqs_per_tile = TOKEN_TILE // dec_seq
    assert dec_batch % seqs_per_tile == 0
    pos_s = np.tile(PAST_LEN + np.arange(dec_seq), seqs_per_tile)
    tables_s = _mix_tables(pos_s, seqs_per_tile, dec_seq)
    cinit_s = jnp.pad(state_conv[0], ((0, 0), (HIST_PAD, 0), (0, 0)))
    h1, route, rnew_s, cnew_s = _mix_call(
        x_sample.reshape(n_sample, D_MODEL), state_ret[0], cinit_s, tables_s, weights, (h1, route),
        n_tok_total=n_tok, n_seq=seqs_per_tile, seq_rows=dec_seq, n_groups=dec_batch // seqs_per_tile,
        n_steps=1, table_per_step=False, tile_offset=n_prompt // TOKEN_TILE, n_fill_groups=0)

    n_tiles = (2 * n_tok) // EXPERT_TILE + N_EXPERTS
    pos, sorted_token, tile_expert, tile_pair_base, n_used = _dispatch_plan(route, n_tiles)
    ys = _expert_call(tile_expert, n_used, tile_pair_base, sorted_token, h1,
                      w_exp_gate[0].reshape(N_EXPERTS, D_MODEL, EXPERT_FF),
                      w_exp_up[0].reshape(N_EXPERTS, D_MODEL, EXPERT_FF),
                      w_exp_down[0].reshape(N_EXPERTS, EXPERT_FF, D_MODEL))
    y_p, y_s = _combine_call(pos, h1, route, p_prompt[0].reshape(n_prompt, PLE_DIM),
                             p_sample[0].reshape(n_sample, PLE_DIM), ys, w_ple_proj[0].astype(bf16),
                             w_ple_gate[0].astype(bf16), row(ln2_g[0]), row(ln2_b[0]))

    return (y_p.reshape(batch, seq, D_MODEL), y_s.reshape(dec_batch, dec_seq, D_MODEL), rnew_p[None],
            cnew_p[None, :, HIST_PAD:, :], rnew_s[None], cnew_s[None, :, HIST_PAD:, :])
```

```python
import functools
import math

import jax
import jax.numpy as jnp
import numpy as np
from jax import lax
from jax.experimental import pallas as pl
from jax.experimental.pallas import tpu as pltpu

D_MODEL = 1024
RET_HEADS = 4
RET_DK = 128
RET_DV = 256
RET_QK = RET_HEADS * RET_DK
RET_V = RET_HEADS * RET_DV
CHUNK = 64
CONV_WIDTH = 31
CONV_HIST = CONV_WIDTH - 1
HIST_ROWS = 32
HIST_PAD = HIST_ROWS - CONV_HIST
N_GROUPS = 4
EXPERTS_PER_GROUP = 8
N_EXPERTS = N_GROUPS * EXPERTS_PER_GROUP
EXPERT_FF = 512
PLE_DIM = 256
PAST_LEN = 4096
LN_EPS = 1e-5
ROPE_BASE = 10000.0
DEPTH = 1
DN_ALPHA = float((2 * DEPTH) ** 0.25)

LANES = 128
TOKEN_TILE = 256
EXPERT_TILE = 256
GATHER_SLOTS = 3
LANE_CHUNKS = D_MODEL // LANES
ROW_TILE_ROWS = D_MODEL // LANES
CONV_ROW_BLOCK = 64
VMEM_LIMIT = 56 * 1024 * 1024

C_Q = 0
C_K = RET_QK
C_V = 2 * RET_QK
C_G = C_V + RET_V
C_GLU_A = C_G + RET_V
C_GLU_B = C_GLU_A + D_MODEL
C_GA = C_GLU_B + D_MODEL
C_GB = C_GA + D_MODEL

ROUTE_COLS = N_GROUPS + N_EXPERTS
PAIR_BITS = 16
NEG_BIG = -1e30
NEG_LOG2_E = -math.log2(math.e)


def _sigmoid(x):
    return 1.0 / (1.0 + jnp.exp2(x * NEG_LOG2_E))


def _normalize(x):
    mu = jnp.mean(x, axis=-1, keepdims=True)
    xc = x - mu
    var = jnp.mean(xc * xc, axis=-1, keepdims=True)
    return xc * lax.rsqrt(var + LN_EPS)


def _layer_norm(x, g, b):
    return _normalize(x) * g + b


def _dot(a, b):
    return jnp.dot(a, b, preferred_element_type=jnp.float32)


def _store_row_tiles(ref, x):
    n = x.shape[0]
    for j in range(ROW_TILE_ROWS):
        ref[pl.ds(j, n, stride=ROW_TILE_ROWS), :] = x[:, j * LANES:(j + 1) * LANES]


def _load_row_tiles(ref, n):
    return jnp.concatenate([ref[pl.ds(j, n, stride=ROW_TILE_ROWS), :] for j in range(ROW_TILE_ROWS)], axis=1)


N_MIX_INPUTS = 27


def _mix_kernel(*refs, n_real_groups, n_fill_groups, aliased, **tile_params):
    if aliased:
        refs = refs[:N_MIX_INPUTS] + refs[N_MIX_INPUTS + 2:]
    h1_ref, route_ref = refs[N_MIX_INPUTS], refs[N_MIX_INPUTS + 1]
    if n_fill_groups == 0:
        _mix_tile(*refs, **tile_params)
        return
    group = pl.program_id(0)

    @pl.when(group < n_real_groups)
    def _():
        _mix_tile(*refs, **tile_params)

    @pl.when(group >= n_real_groups)
    def _():
        h1_ref[...] = jnp.zeros_like(h1_ref)
        route_ref[...] = jnp.zeros_like(route_ref)


def _mix_tile(x_ref, rinit_ref, cinit_ref, cq_ref, sq_ref, ck_ref, sk_ref,
              xi_ref, zeta_ref, mask_ref,
              ln_in_g_ref, ln_in_b_ref, w_in_ref, w_ret_out_ref, conv_w_ref, conv_b_ref,
              conv_ln_g_ref, conv_ln_b_ref, w_conv_out_ref, w_out_ref, ln1_g_ref, ln1_b_ref,
              w_route_ref, w_route_hi_ref, b_route_ref, earlier_ref, count_in_ref,
              h1_ref, route_ref, rnew_ref, cnew_ref, count_out_ref,
              r_scr, ubuf, c_scr, count_scr,
              *, n_seq, seq_rows, state_decay):
    @pl.when((pl.program_id(0) == 0) & (pl.program_id(1) == 0))
    def _():
        count_scr[...] = count_in_ref[...]

    t = pl.program_id(1)
    last_t = pl.num_programs(1) - 1
    bf16 = jnp.bfloat16

    @pl.when(t == 0)
    def _():
        r_scr[...] = rinit_ref[...]
        for j in range(LANE_CHUNKS):
            ubuf[:, j, 0:HIST_ROWS, :] = cinit_ref[:, :, j * LANES:(j + 1) * LANES]

    h = _layer_norm(x_ref[...], ln_in_g_ref[...], ln_in_b_ref[...])
    hb = h.astype(bf16)

    u = _dot(hb, w_in_ref[:, C_GLU_A:C_GLU_A + D_MODEL]) * _sigmoid(
        _dot(hb, w_in_ref[:, C_GLU_B:C_GLU_B + D_MODEL]))
    for sq_i in range(n_seq):
        for j in range(LANE_CHUNKS):
            ubuf[sq_i, j, HIST_ROWS:HIST_ROWS + seq_rows, :] = u[sq_i * seq_rows:(sq_i + 1) * seq_rows,
                                                                 j * LANES:(j + 1) * LANES]

    row_block = min(CONV_ROW_BLOCK, seq_rows)

    def conv_lane_chunk(j):
        lane = slice(j * LANES, (j + 1) * LANES)
        for sq_i in range(n_seq):
            for rb in range(seq_rows // row_block):
                r0 = rb * row_block + HIST_PAD
                acc = None
                for kk in range(CONV_WIDTH):
                    term = conv_w_ref[kk:kk + 1, lane] * ubuf[sq_i, j, r0 + kk:r0 + kk + row_block, :]
                    acc = term if acc is None else acc + term
                out_row = sq_i * seq_rows + rb * row_block
                c_scr[out_row:out_row + row_block, lane] = acc

    assert LANE_CHUNKS == 2 * RET_HEADS

    q = _dot(hb, w_in_ref[:, C_Q:C_Q + RET_QK])
    conv_lane_chunk(0)
    k = _dot(hb, w_in_ref[:, C_K:C_K + RET_QK])
    conv_lane_chunk(1)
    v = _dot(hb, w_in_ref[:, C_V:C_V + RET_V])
    conv_lane_chunk(2)
    g = _dot(hb, w_in_ref[:, C_G:C_G + RET_V])
    g = g * _sigmoid(g)
    ga = _sigmoid(_dot(hb, w_in_ref[:, C_GA:C_GA + D_MODEL]))
    conv_lane_chunk(3)
    gb = _sigmoid(_dot(hb, w_in_ref[:, C_GB:C_GB + D_MODEL]))
    cq, sq, ck, sk = cq_ref[...], sq_ref[...], ck_ref[...], sk_ref[...]

    y_ret = None
    for hd in range(RET_HEADS):
        qh = q[:, hd * RET_DK:(hd + 1) * RET_DK]
        kh = k[:, hd * RET_DK:(hd + 1) * RET_DK]
        qr = qh * cq + pltpu.roll(qh, RET_DK // 2, 1) * sq
        kr = kh * ck + pltpu.roll(kh, RET_DK // 2, 1) * sk
        vb = v[:, hd * RET_DV:(hd + 1) * RET_DV].astype(bf16)
        s = lax.dot_general(qr.astype(bf16), kr.astype(bf16), (((1,), (1,)), ((), ())),
                            preferred_element_type=jnp.float32)
        inner = _dot((s * mask_ref[hd]).astype(bf16), vb)
        qx = (qr * xi_ref[hd]).astype(bf16)
        kz = (kr * zeta_ref[hd]).astype(bf16)
        o_parts = []
        for sq_i in range(n_seq):
            rows = slice(sq_i * seq_rows, (sq_i + 1) * seq_rows)
            r_old = r_scr[sq_i, hd]
            o_parts.append(inner[rows] + _dot(qx[rows], r_old.astype(bf16)))
            kv = lax.dot_general(kz[rows], vb[rows], (((0,), (0,)), ((), ())),
                                 preferred_element_type=jnp.float32)
            r_scr[sq_i, hd] = state_decay[hd] * r_old + kv
        o = o_parts[0] if n_seq == 1 else jnp.concatenate(o_parts, axis=0)
        on = _normalize(o)
        gated = (on * g[:, hd * RET_DV:(hd + 1) * RET_DV]).astype(bf16)
        part = _dot(gated, w_ret_out_ref[hd * RET_DV:(hd + 1) * RET_DV, :])
        y_ret = part if y_ret is None else y_ret + part
        conv_lane_chunk(RET_HEADS + hd)

    for sq_i in range(n_seq):
        for j in range(LANE_CHUNKS):
            ubuf[sq_i, j, 0:HIST_ROWS, :] = ubuf[sq_i, j, seq_rows:seq_rows + HIST_ROWS, :]

    c = _layer_norm(c_scr[...] + conv_b_ref[...], conv_ln_g_ref[...], conv_ln_b_ref[...])
    c = c * _sigmoid(c)
    y_conv = _dot(c.astype(bf16), w_conv_out_ref[...])

    merged =(ga * y_ret + gb * y_conv).astype(bf16)
    h1 = _layer_norm(DN_ALPHA * h + _dot(merged, w_out_ref[...]), ln1_g_ref[...], ln1_b_ref[...])
    _store_row_tiles(h1_ref, h1)

    h1_hi = h1.astype(bf16)
    h1_lo = (h1 - h1_hi.astype(jnp.float32)).astype(bf16)
    both = _dot(h1_hi, w_route_ref[...])
    logits = both[:, :LANES] + both[:, LANES:] + _dot(h1_lo, w_route_hi_ref[...]) + b_route_ref[...]
    lane = lax.broadcasted_iota(jnp.int32, logits.shape, 1)
    lane_f = lane.astype(jnp.float32)
    is_group = lane < N_GROUPS
    gl = jnp.where(is_group, logits, NEG_BIG)
    gmax = jnp.max(gl, axis=-1, keepdims=True)
    gexp = jnp.where(is_group, jnp.exp(gl - gmax), 0.0)
    gprob = gexp / jnp.sum(gexp, axis=-1, keepdims=True)
    gp = jnp.max(gprob, axis=-1, keepdims=True)
    gi = jnp.min(jnp.where(is_group & (gprob == gp), lane_f, float(LANES)), axis=-1, keepdims=True)
    e_lo = N_GROUPS + EXPERTS_PER_GROUP * gi
    in_group = (lane_f >= e_lo) & (lane_f < e_lo + EXPERTS_PER_GROUP)
    el = jnp.where(in_group, logits, NEG_BIG)
    ev1 = jnp.max(el, axis=-1, keepdims=True)
    ei1 = jnp.min(jnp.where(in_group & (el == ev1), lane_f, float(LANES)), axis=-1, keepdims=True)
    rest = in_group & (lane_f != ei1)
    el2 = jnp.where(rest, logits, NEG_BIG)
    ev2 = jnp.max(el2, axis=-1, keepdims=True)
    ei2 = jnp.min(jnp.where(rest & (el2 == ev2), lane_f, float(LANES)), axis=-1, keepdims=True)
    x2 = jnp.exp(ev2 - ev1)
    w1 = gp / (1.0 + x2)
    w2 = w1 * x2
    e1 = ei1 - N_GROUPS
    e2 = ei2 - N_GROUPS

    chosen = jnp.where((lane_f == e1) | (lane_f == e2), 1.0, 0.0)
    before = _dot(earlier_ref[...], chosen.astype(bf16)) + count_scr[0:1, :]
    rank1 = jnp.sum(jnp.where(lane_f == e1, before, 0.0), axis=-1, keepdims=True)
    rank2 = jnp.sum(jnp.where(lane_f == e2, before, 0.0), axis=-1, keepdims=True)
    count_scr[...] = count_scr[...] + jnp.sum(chosen, axis=0, keepdims=True)
    count_out_ref[...] = count_scr[...]

    route = jnp.where(lane == 0, e1, 0.0)
    route = jnp.where(lane == 1, e2, route)
    route = jnp.where(lane == 2, w1, route)
    route = jnp.where(lane == 3, w2, route)
    route = jnp.where(lane == 4, rank1, route)
    route = jnp.where(lane == 5, rank2, route)
    route_ref[...] = route

    @pl.when(t == last_t)
    def _():
        rnew_ref[...] = r_scr[...]
        for j in range(LANE_CHUNKS):
            cnew_ref[:, :, j * LANES:(j + 1) * LANES] = ubuf[:, j, 0:HIST_ROWS, :]


def _const_spec(shape):
    zeros = (0,) * len(shape)
    return pl.BlockSpec(shape, lambda b, t: zeros, pipeline_mode=pl.Buffered(1))


def _mix_call(x, rinit, cinit, tables, weights, shared, count_in, *, n_tok_total, n_seq, seq_rows, n_groups, n_steps,
              table_per_step, tile_offset, n_fill_groups):
    tt = n_seq * seq_rows
    assert tt == TOKEN_TILE and x.shape[0] == n_groups * n_steps * tt
    log_decay = [math.log1p(-(2.0 ** (-5.0 - hd))) for hd in range(RET_HEADS)]
    state_decay = tuple(math.exp(lg * seq_rows) for lg in log_decay)
    cq, sq, ck, sk, xi, zeta, mask = tables
    last_group = n_groups - 1
    own = lambda b: jnp.minimum(b, last_group)

    in_tok_spec = pl.BlockSpec((tt, D_MODEL), lambda b, t: (own(b) * n_steps + t, 0))
    out_tok_spec = lambda rows: pl.BlockSpec((rows, LANES), lambda b, t: (tile_offset + b * n_steps + t, 0))
    rope_spec = pl.BlockSpec((tt, LANES), (lambda b, t: (t, 0)) if table_per_step else (lambda b, t: (0, 0)))
    state_r_spec = pl.BlockSpec((n_seq, RET_HEADS, RET_DK, RET_DV), lambda b, t: (own(b), 0, 0, 0))
    state_c_spec = pl.BlockSpec((n_seq, HIST_ROWS, D_MODEL), lambda b, t: (own(b), 0, 0))
    any_spec = pl.BlockSpec(memory_space=pl.ANY)

    in_specs = [in_tok_spec, state_r_spec, state_c_spec, rope_spec, rope_spec, rope_spec, rope_spec,
                _const_spec(xi.shape), _const_spec(zeta.shape), _const_spec(mask.shape)]
    earlier = jnp.asarray(np.tril(np.ones((tt, tt), np.float32), -1), jnp.bfloat16)
    weights = tuple(weights) + (earlier, count_in)
    in_specs += [_const_spec(w.shape) for w in weights]
    assert len(in_specs) == N_MIX_INPUTS
    aliases = {}
    extra = ()
    if shared is not None:
        assert n_fill_groups == 0
        in_specs += [any_spec, any_spec]
        aliases = {N_MIX_INPUTS: 0, N_MIX_INPUTS + 1: 1}
        extra = tuple(shared)
    out_shape = (jax.ShapeDtypeStruct((n_tok_total * ROW_TILE_ROWS, LANES), jnp.float32),
                 jax.ShapeDtypeStruct((n_tok_total, LANES), jnp.float32),
                 jax.ShapeDtypeStruct(rinit.shape, jnp.float32),
                 jax.ShapeDtypeStruct(cinit.shape, jnp.float32),
                 jax.ShapeDtypeStruct(count_in.shape, jnp.float32))
    out_specs = (out_tok_spec(tt * ROW_TILE_ROWS), out_tok_spec(tt), state_r_spec, state_c_spec,
                 pl.BlockSpec(count_in.shape, lambda b, t: (0, 0)))
    body = functools.partial(_mix_kernel, n_real_groups=n_groups, n_fill_groups=n_fill_groups,
                             aliased=shared is not None,
                             n_seq=n_seq, seq_rows=seq_rows, state_decay=state_decay)
    return pl.pallas_call(
        body,
        grid=(n_groups + n_fill_groups, n_steps),
        in_specs=in_specs,
        out_specs=out_specs,
        out_shape=out_shape,
        input_output_aliases=aliases,
        scratch_shapes=[pltpu.VMEM((n_seq, RET_HEADS, RET_DK, RET_DV), jnp.float32),
                        pltpu.VMEM((n_seq, LANE_CHUNKS, HIST_ROWS + seq_rows, LANES), jnp.float32),
                        pltpu.VMEM((tt, D_MODEL), jnp.float32),
                        pltpu.VMEM(count_in.shape, jnp.float32)],
        compiler_params=pltpu.CompilerParams(dimension_semantics=("arbitrary", "arbitrary"),
                                             vmem_limit_bytes=VMEM_LIMIT),
        name="mix",
    )(x, rinit, cinit, cq, sq, ck, sk, xi, zeta, mask, *weights, *extra)


def _mix_tables(positions, n_seq, seq_rows):
    half = RET_DK // 2
    inv = ROPE_BASE ** (-np.arange(half, dtype=np.float64) / half)
    ang = np.asarray(positions, np.float64)[:, None] * inv[None, :]
    cos, sin = np.cos(ang), np.sin(ang)
    cq = np.concatenate([cos, cos], axis=1)
    sq = np.concatenate([-sin, sin], axis=1)
    scale = RET_DK ** -0.5
    lg = np.log1p(-np.exp2(-5.0 - np.arange(RET_HEADS, dtype=np.float64)))
    tt = n_seq * seq_rows
    i = np.arange(tt)
    loc = (i % seq_rows).astype(np.float64)
    xi = np.exp(lg[:, None] * (loc + 1.0)[None, :])
    zeta = np.exp(lg[:, None] * (seq_rows - 1.0 - loc)[None, :])
    xi = np.broadcast_to(xi[:, :, None], (RET_HEADS, tt, LANES))
    zeta = np.broadcast_to(zeta[:, :, None], (RET_HEADS, tt, LANES))
    same_seq = (i[:, None] // seq_rows) == (i[None, :] // seq_rows)
    visible = same_seq & ((i[None, :] // CHUNK) <= (i[:, None] // CHUNK))
    dist = np.abs(i[:, None] - i[None, :]).astype(np.float64)
    mask = np.where(visible[None], np.exp(lg[:, None, None] * dist[None]), 0.0)
    tables = (cq, sq, cq * scale, sq * scale, xi, zeta, mask)
    return tuple(jnp.asarray(np.ascontiguousarray(tab), jnp.float32) for tab in tables)


def _expert_kernel(tile_expert_ref, n_used_ref, tile_pair_base_ref, sorted_token_ref, h1_hbm, wg_ref, wu_ref,
                   wd_ref, ys_ref, xbuf, sems, wg_b, wu_b, wd_b):
    i = pl.program_id(0)
    n_used = n_used_ref[0]
    bf16 = jnp.bfloat16
    slot = lax.rem(i, GATHER_SLOTS)

    def start_gather(tile):
        dst_slot = lax.rem(tile, GATHER_SLOTS)
        pair_base = tile_pair_base_ref[tile]

        def issue(r, carry):
            src = pl.multiple_of(sorted_token_ref[pair_base + r] * ROW_TILE_ROWS, ROW_TILE_ROWS)
            dst = pl.multiple_of(r * ROW_TILE_ROWS, ROW_TILE_ROWS)
            pltpu.make_async_copy(h1_hbm.at[pl.ds(src, ROW_TILE_ROWS)],
                                  xbuf.at[dst_slot, pl.ds(dst, ROW_TILE_ROWS)],
                                  sems.at[dst_slot]).start(priority=1)
            return carry
        lax.fori_loop(0, EXPERT_TILE, issue, 0, unroll=8)

    def wait_gather(dst_slot):
        pltpu.make_async_copy(h1_hbm.at[pl.ds(0, EXPERT_TILE * ROW_TILE_ROWS)], xbuf.at[dst_slot],
                              sems.at[dst_slot]).wait()

    for first in range(GATHER_SLOTS - 1):
        @pl.when((i == 0) & (first < n_used))
        def _(first=first):
            start_gather(first)

    @pl.when(i + GATHER_SLOTS - 1 < n_used)
    def _():
        start_gather(i + GATHER_SLOTS - 1)

    prev = tile_expert_ref[jnp.maximum(i - 1, 0)]
    new_expert = (i == 0) | (tile_expert_ref[i] != prev)

    @pl.when(new_expert)
    def _():
        wg_b[...] = wg_ref[0].astype(bf16)
        wu_b[...] = wu_ref[0].astype(bf16)
        wd_b[...] = wd_ref[0].astype(bf16)

    @pl.when(i < n_used)
    def _():
        wait_gather(slot)
        x = _load_row_tiles(xbuf.at[slot], EXPERT_TILE).astype(bf16)
        gate = _dot(x, wg_b[...])
        up = _dot(x, wu_b[...])
        act = (gate * _sigmoid(gate) * up).astype(bf16)
        _store_row_tiles(ys_ref, _dot(act, wd_b[...]))

    @pl.when(i >= n_used)
    def _():
        ys_ref[...] = jnp.zeros_like(ys_ref)


def _expert_call(tile_expert, n_used, tile_pair_base, sorted_token, h1, wg, wu, wd):
    n_tiles = tile_expert.shape[0]
    w_map = lambda i, te, nu, pb, st: (te[i], 0, 0)
    grid_spec = pltpu.PrefetchScalarGridSpec(
        num_scalar_prefetch=4,
        grid=(n_tiles,),
        in_specs=[pl.BlockSpec(memory_space=pl.ANY),
                  pl.BlockSpec((1, D_MODEL, EXPERT_FF), w_map),
                  pl.BlockSpec((1, D_MODEL, EXPERT_FF), w_map),
                  pl.BlockSpec((1, EXPERT_FF, D_MODEL), w_map)],
        out_specs=pl.BlockSpec((EXPERT_TILE * ROW_TILE_ROWS, LANES), lambda i, te, nu, pb, st: (i, 0)),
        scratch_shapes=[pltpu.VMEM((GATHER_SLOTS, EXPERT_TILE * ROW_TILE_ROWS, LANES), jnp.float32),
                        pltpu.SemaphoreType.DMA((GATHER_SLOTS,)),
                        pltpu.VMEM((D_MODEL, EXPERT_FF), jnp.bfloat16),
                        pltpu.VMEM((D_MODEL, EXPERT_FF), jnp.bfloat16),
                        pltpu.VMEM((EXPERT_FF, D_MODEL), jnp.bfloat16)],
    )
    return pl.pallas_call(
        _expert_kernel,
        grid_spec=grid_spec,
        out_shape=jax.ShapeDtypeStruct((n_tiles * EXPERT_TILE * ROW_TILE_ROWS, LANES), jnp.float32),
        compiler_params=pltpu.CompilerParams(dimension_semantics=("arbitrary",),
                                             vmem_limit_bytes=VMEM_LIMIT),
        name="experts",
    )(tile_expert, n_used, tile_pair_base, sorted_token, h1, wg, wu, wd)


def _combine_kernel(pos_ref, h1_ref, route_ref, pp_ref, ps_ref, ys_hbm, w_ple_proj_ref, w_ple_gate_ref,
                    ln2_g_ref, ln2_b_ref, yp_ref, ys_out_ref, ybuf, sems, *, n_prompt_tiles):
    i = pl.program_id(0)
    n_steps = pl.num_programs(0)
    bf16 = jnp.bfloat16
    slot = lax.rem(i, 2)

    def start_gather(tile, dst_slot):
        def issue(r, carry):
            pair = 2 * (tile * TOKEN_TILE + r)
            dst = pl.multiple_of(r * ROW_TILE_ROWS, ROW_TILE_ROWS)
            for which in range(2):
                src = pl.multiple_of(pos_ref[pair + which] * ROW_TILE_ROWS, ROW_TILE_ROWS)
                pltpu.make_async_copy(ys_hbm.at[pl.ds(src, ROW_TILE_ROWS)],
                                      ybuf.at[dst_slot, which, pl.ds(dst, ROW_TILE_ROWS)],
                                      sems.at[dst_slot]).start(priority=which)
            return carry
        lax.fori_loop(0, TOKEN_TILE, issue, 0, unroll=4)

    def wait_gather(dst_slot):
        for which in range(2):
            pltpu.make_async_copy(ys_hbm.at[pl.ds(0, TOKEN_TILE * ROW_TILE_ROWS)], ybuf.at[dst_slot, which],
                                  sems.at[dst_slot]).wait()

    @pl.when(i == 0)
    def _():
        start_gather(0, 0)

    @pl.when(i + 1 < n_steps)
    def _():
        start_gather(i + 1, 1 - slot)

    h1 = _load_row_tiles(h1_ref, TOKEN_TILE)
    is_prompt = i < n_prompt_tiles
    p = jnp.where(is_prompt, pp_ref[...], ps_ref[...])
    pe = _dot(p.astype(bf16), w_ple_proj_ref[...]) * _sigmoid(_dot(h1.astype(bf16), w_ple_gate_ref[...]))
    route = route_ref[...]
    w1 = route[:, 2:3]
    w2 = route[:, 3:4]
    wait_gather(slot)
    y1 = _load_row_tiles(ybuf.at[slot, 0], TOKEN_TILE)
    y2 = _load_row_tiles(ybuf.at[slot, 1], TOKEN_TILE)
    total = DN_ALPHA * h1 + pe + w1 * y1 + w2 * y2
    out = _layer_norm(total, ln2_g_ref[...], ln2_b_ref[...])

    @pl.when(is_prompt)
    def _():
        yp_ref[...] = out

    @pl.when(jnp.logical_not(is_prompt))
    def _():
        ys_out_ref[...] = out


def _combine_call(pos, h1, route, p_prompt, p_sample, ys, w_ple_proj, w_ple_gate, ln2_g, ln2_b):
    n_tok = route.shape[0]
    n_prompt_tiles = p_prompt.shape[0] // TOKEN_TILE
    n_sample_tiles = p_sample.shape[0] // TOKEN_TILE
    n_steps = n_tok // TOKEN_TILE
    assert n_steps == n_prompt_tiles + n_sample_tiles
    tok_spec = lambda cols: pl.BlockSpec((TOKEN_TILE, cols), lambda i, pos: (i, 0))
    prompt_spec = lambda cols: pl.BlockSpec((TOKEN_TILE, cols),
                                            lambda i, pos: (jnp.minimum(i, n_prompt_tiles - 1), 0))
    sample_spec = lambda cols: pl.BlockSpec((TOKEN_TILE, cols),
                                            lambda i, pos: (jnp.maximum(i - n_prompt_tiles, 0), 0))
    const_spec = lambda shape: pl.BlockSpec(shape, lambda i, pos: (0, 0))
    grid_spec = pltpu.PrefetchScalarGridSpec(
        num_scalar_prefetch=1,
        grid=(n_steps,),
        in_specs=[pl.BlockSpec((TOKEN_TILE * ROW_TILE_ROWS, LANES), lambda i, pos: (i, 0)), tok_spec(LANES),
                  prompt_spec(PLE_DIM), sample_spec(PLE_DIM),
                  pl.BlockSpec(memory_space=pl.ANY), const_spec(w_ple_proj.shape), const_spec(w_ple_gate.shape),
                  const_spec(ln2_g.shape), const_spec(ln2_b.shape)],
        out_specs=(prompt_spec(D_MODEL), sample_spec(D_MODEL)),
        scratch_shapes=[pltpu.VMEM((2, 2, TOKEN_TILE * ROW_TILE_ROWS, LANES), jnp.float32),
                        pltpu.SemaphoreType.DMA((2,))],
    )
    return pl.pallas_call(
        functools.partial(_combine_kernel, n_prompt_tiles=n_prompt_tiles),
        grid_spec=grid_spec,
        out_shape=(jax.ShapeDtypeStruct((n_prompt_tiles * TOKEN_TILE, D_MODEL), jnp.float32),
                   jax.ShapeDtypeStruct((n_sample_tiles * TOKEN_TILE, D_MODEL), jnp.float32)),
        compiler_params=pltpu.CompilerParams(dimension_semantics=("arbitrary",),
                                             vmem_limit_bytes=VMEM_LIMIT),
        name="combine",
    )(pos, h1, route, p_prompt, p_sample, ys, w_ple_proj, w_ple_gate, ln2_g, ln2_b)


def _dispatch_plan(route, pair_count, n_tiles):
    experts = route[:, 0:2].astype(jnp.int32).reshape(-1)
    rank = route[:, 4:6].astype(jnp.int32).reshape(-1)
    n_pairs = experts.shape[0]
    counts = pair_count[0, :N_EXPERTS].astype(jnp.int32)
    tiles_per = (counts + EXPERT_TILE - 1) // EXPERT_TILE
    tile_end = jnp.cumsum(tiles_per)
    row_start = (tile_end - tiles_per) * EXPERT_TILE
    pos = row_start[experts] + rank
    n_used = tile_end[-1]
    tile_index = jnp.arange(n_tiles, dtype=jnp.int32)
    tile_expert = jnp.sum((jnp.minimum(tile_index, n_used - 1)[:, None] >= tile_end[None, :]).astype(jnp.int32),
                          axis=1)

    assert n_pairs <= 1 << PAIR_BITS
    keys = jnp.sort(experts * (1 << PAIR_BITS) + jnp.arange(n_pairs, dtype=jnp.int32))
    sorted_token = jnp.concatenate([(keys & ((1 << PAIR_BITS) - 1)) >> 1,
                                    jnp.zeros((EXPERT_TILE,), jnp.int32)])
    pair_start = jnp.cumsum(counts) - counts
    tile_pair_base = jnp.clip(pair_start[tile_expert] + tile_index * EXPERT_TILE - row_start[tile_expert],
                              0, n_pairs)
    return (pos.astype(jnp.int32), sorted_token, tile_expert.astype(jnp.int32),
            tile_pair_base.astype(jnp.int32), n_used.reshape(1).astype(jnp.int32))


def kernel(x_prompt, x_sample, p_prompt, p_sample, state_ret, state_conv, ln_in_g, ln_in_b, w_in, w_ret_out,
           conv_w, conv_b, conv_ln_g, conv_ln_b, w_conv_out, w_out, ln1_g, ln1_b, w_route_g, b_route_g,
           w_route_e, b_route_e, w_exp_gate, w_exp_up, w_exp_down, w_ple_proj, w_ple_gate, ln2_g, ln2_b):
    assert w_in.shape[0] == DEPTH == 1
    bf16 = jnp.bfloat16
    batch, seq, _ = x_prompt.shape
    dec_batch, dec_seq, _ = x_sample.shape
    assert dec_seq == CHUNK and seq % TOKEN_TILE == 0 and TOKEN_TILE % CHUNK == 0
    row = lambda a: a.reshape(1, -1)
    n_prompt = batch * seq
    n_sample = dec_batch * dec_seq
    n_tok = n_prompt + n_sample

    w_route = jnp.concatenate(
        [w_route_g[0], jnp.transpose(w_route_e[0], (1, 0, 2)).reshape(D_MODEL, N_EXPERTS),
         jnp.zeros((D_MODEL, LANES - ROUTE_COLS), jnp.float32)], axis=1)
    w_route_hi = w_route.astype(bf16)
    w_route_lo = (w_route - w_route_hi.astype(jnp.float32)).astype(bf16)
    b_route = jnp.concatenate([b_route_g[0], b_route_e[0].reshape(-1),
                               jnp.zeros((LANES - ROUTE_COLS,), jnp.float32)]).reshape(1, LANES)
    conv_w_pad = jnp.concatenate([conv_w[0], jnp.zeros((HIST_ROWS - CONV_WIDTH, D_MODEL), jnp.float32)], axis=0)
    weights = (row(ln_in_g), row(ln_in_b), w_in[0].astype(bf16), w_ret_out[0].astype(bf16), conv_w_pad,
               row(conv_b[0]), row(conv_ln_g[0]), row(conv_ln_b[0]), w_conv_out[0].astype(bf16),
               w_out[0].astype(bf16), row(ln1_g[0]), row(ln1_b[0]),
               jnp.concatenate([w_route_hi, w_route_lo], axis=1), w_route_hi, b_route)

    steps_p = seq // TOKEN_TILE
    sample_tiles = n_sample // TOKEN_TILE
    assert sample_tiles % steps_p == 0
    tables_p = _mix_tables(np.arange(seq), 1, TOKEN_TILE)
    rinit_p = jnp.zeros((batch, RET_HEADS, RET_DK, RET_DV), jnp.float32)
    cinit_p = jnp.zeros((batch, HIST_ROWS, D_MODEL), jnp.float32)
    h1, route, rnew_p, cnew_p, pair_count = _mix_call(
        x_prompt.reshape(n_prompt, D_MODEL), rinit_p, cinit_p, tables_p, weights, None,
        jnp.zeros((8, LANES), jnp.float32), n_tok_total=n_tok,
        n_seq=1, seq_rows=TOKEN_TILE, n_groups=batch, n_steps=steps_p, table_per_step=True,
        tile_offset=0, n_fill_groups=sample_tiles // steps_p)

    seqs_per_tile = TOKEN_TILE // dec_seq
    assert dec_batch % seqs_per_tile == 0
    pos_s = np.tile(PAST_LEN + np.arange(dec_seq), seqs_per_tile)
    tables_s = _mix_tables(pos_s, seqs_per_tile, dec_seq)
    cinit_s = jnp.pad(state_conv[0], ((0, 0), (HIST_PAD, 0), (0, 0)))
    h1, route, rnew_s, cnew_s, pair_count = _mix_call(
        x_sample.reshape(n_sample, D_MODEL), state_ret[0], cinit_s, tables_s, weights, (h1, route), pair_count,
        n_tok_total=n_tok, n_seq=seqs_per_tile, seq_rows=dec_seq, n_groups=dec_batch // seqs_per_tile,
        n_steps=1, table_per_step=False, tile_offset=n_prompt // TOKEN_TILE, n_fill_groups=0)

    n_tiles = (2 * n_tok) // EXPERT_TILE + N_EXPERTS
    pos, sorted_token, tile_expert, tile_pair_base, n_used = _dispatch_plan(route, pair_count, n_tiles)
    ys = _expert_call(tile_expert, n_used, tile_pair_base, sorted_token, h1,
                      w_exp_gate[0].reshape(N_EXPERTS, D_MODEL, EXPERT_FF),
                      w_exp_up[0].reshape(N_EXPERTS, D_MODEL, EXPERT_FF),
                      w_exp_down[0].reshape(N_EXPERTS, EXPERT_FF, D_MODEL))
    y_p, y_s = _combine_call(pos, h1, route, p_prompt[0].reshape(n_prompt, PLE_DIM),
                             p_sample[0].reshape(n_sample, PLE_DIM), ys, w_ple_proj[0].astype(bf16),
                             w_ple_gate[0].astype(bf16), row(ln2_g[0]), row(ln2_b[0]))

    return (y_p.reshape(batch, seq, D_MODEL), y_s.reshape(dec_batch, dec_seq, D_MODEL), rnew_p[None],
            cnew_p[None, :, HIST_PAD:, :], rnew_s[None], cnew_s[None, :, HIST_PAD:, :])
```

```python
import functools
import math

import jax
import jax.numpy as jnp
import numpy as np
from jax import lax
from jax.experimental import pallas as pl
from jax.experimental.pallas import tpu as pltpu

D_MODEL = 1024
RET_HEADS = 4
RET_DK = 128
RET_DV = 256
RET_QK = RET_HEADS * RET_DK
RET_V = RET_HEADS * RET_DV
CHUNK = 64
CONV_WIDTH = 31
CONV_HIST = CONV_WIDTH - 1
HIST_ROWS = 32
HIST_PAD = HIST_ROWS - CONV_HIST
N_GROUPS = 4
EXPERTS_PER_GROUP = 8
N_EXPERTS = N_GROUPS * EXPERTS_PER_GROUP
EXPERT_FF = 512
PLE_DIM = 256
PAST_LEN = 4096
LN_EPS = 1e-5
ROPE_BASE = 10000.0
DEPTH = 1
DN_ALPHA = float((2 * DEPTH) ** 0.25)

LANES = 128
TOKEN_TILE = 256
EXPERT_TILE = 256
GATHER_SLOTS = 3
LANE_CHUNKS = D_MODEL // LANES
ROW_TILE_ROWS = D_MODEL // LANES
CONV_ROW_BLOCK = 64
VMEM_LIMIT = 56 * 1024 * 1024

C_Q = 0
C_K = RET_QK
C_V = 2 * RET_QK
C_G = C_V + RET_V
C_GLU_A = C_G + RET_V
C_GLU_B = C_GLU_A + D_MODEL
C_GA = C_GLU_B + D_MODEL
C_GB = C_GA + D_MODEL

ROUTE_COLS = N_GROUPS + N_EXPERTS
PAIR_BITS = 16
NEG_BIG = -1e30
NEG_LOG2_E = -math.log2(math.e)


def _sigmoid(x):
    return 1.0 / (1.0 + jnp.exp2(x * NEG_LOG2_E))


def _normalize(x):
    mu = jnp.mean(x, axis=-1, keepdims=True)
    xc = x - mu
    var = jnp.mean(xc * xc, axis=-1, keepdims=True)
    return xc * lax.rsqrt(var + LN_EPS)


def _layer_norm(x, g, b):
    return _normalize(x) * g + b


def _dot(a, b):
    return jnp.dot(a, b, preferred_element_type=jnp.float32)


def _store_row_tiles(ref, x):
    n = x.shape[0]
    for j in range(ROW_TILE_ROWS):
        ref[pl.ds(j, n, stride=ROW_TILE_ROWS), :] = x[:, j * LANES:(j + 1) * LANES]


def _load_row_tiles(ref, n):
    return jnp.concatenate([ref[pl.ds(j, n, stride=ROW_TILE_ROWS), :] for j in range(ROW_TILE_ROWS)], axis=1)


N_MIX_INPUTS = 25


def _mix_kernel(*refs, n_real_groups, n_fill_groups, aliased, **tile_params):
    if aliased:
        refs = refs[:N_MIX_INPUTS] + refs[N_MIX_INPUTS + 2:]
    h1_ref, route_ref = refs[N_MIX_INPUTS], refs[N_MIX_INPUTS + 1]
    if n_fill_groups == 0:
        _mix_tile(*refs, **tile_params)
        return
    group = pl.program_id(0)

    @pl.when(group < n_real_groups)
    def _():
        _mix_tile(*refs, **tile_params)

    @pl.when(group >= n_real_groups)
    def _():
        h1_ref[...] = jnp.zeros_like(h1_ref)
        route_ref[...] = jnp.zeros_like(route_ref)


def _mix_tile(x_ref, rinit_ref, cinit_ref, cq_ref, sq_ref, ck_ref, sk_ref,
              xi_ref, zeta_ref, mask_ref,
              ln_in_g_ref, ln_in_b_ref, w_in_ref, w_ret_out_ref, conv_w_ref, conv_b_ref,
              conv_ln_g_ref, conv_ln_b_ref, w_conv_out_ref, w_out_ref, ln1_g_ref, ln1_b_ref,
              w_route_ref, w_route_hi_ref, b_route_ref,
              h1_ref, route_ref, rnew_ref, cnew_ref,
              r_scr, ubuf, c_scr,
              *, n_seq, seq_rows, state_decay):
    t = pl.program_id(1)
    last_t = pl.num_programs(1) - 1
    bf16 = jnp.bfloat16

    @pl.when(t == 0)
    def _():
        r_scr[...] = rinit_ref[...]
        for j in range(LANE_CHUNKS):
            ubuf[:, j, 0:HIST_ROWS, :] = cinit_ref[:, :, j * LANES:(j + 1) * LANES]

    h = _layer_norm(x_ref[...], ln_in_g_ref[...], ln_in_b_ref[...])
    hb = h.astype(bf16)

    u = _dot(hb, w_in_ref[:, C_GLU_A:C_GLU_A + D_MODEL]) * _sigmoid(
        _dot(hb, w_in_ref[:, C_GLU_B:C_GLU_B + D_MODEL]))
    for sq_i in range(n_seq):
        for j in range(LANE_CHUNKS):
            ubuf[sq_i, j, HIST_ROWS:HIST_ROWS + seq_rows, :] = u[sq_i * seq_rows:(sq_i + 1) * seq_rows,
                                                                 j * LANES:(j + 1) * LANES]

    row_block = min(CONV_ROW_BLOCK, seq_rows)

    def conv_lane_chunk(j):
        lane = slice(j * LANES, (j + 1) * LANES)
        for sq_i in range(n_seq):
            for rb in range(seq_rows // row_block):
                r0 = rb * row_block + HIST_PAD
                acc = None
                for kk in range(CONV_WIDTH):
                    term = conv_w_ref[kk:kk + 1, lane] * ubuf[sq_i, j, r0 + kk:r0 + kk + row_block, :]
                    acc = term if acc is None else acc + term
                out_row = sq_i * seq_rows + rb * row_block
                c_scr[out_row:out_row + row_block, lane] = acc

    assert LANE_CHUNKS == 2 * RET_HEADS

    q = _dot(hb, w_in_ref[:, C_Q:C_Q + RET_QK])
    conv_lane_chunk(0)
    k = _dot(hb, w_in_ref[:, C_K:C_K + RET_QK])
    conv_lane_chunk(1)
    v = _dot(hb, w_in_ref[:, C_V:C_V + RET_V])
    conv_lane_chunk(2)
    g = _dot(hb, w_in_ref[:, C_G:C_G + RET_V])
    g = g * _sigmoid(g)
    ga = _sigmoid(_dot(hb, w_in_ref[:, C_GA:C_GA + D_MODEL]))
    conv_lane_chunk(3)
    gb = _sigmoid(_dot(hb, w_in_ref[:, C_GB:C_GB + D_MODEL]))
    cq, sq, ck, sk = cq_ref[...], sq_ref[...], ck_ref[...], sk_ref[...]

    y_ret = None
    for hd in range(RET_HEADS):
        qh = q[:, hd * RET_DK:(hd + 1) * RET_DK]
        kh = k[:, hd * RET_DK:(hd + 1) * RET_DK]
        qr = qh * cq + pltpu.roll(qh, RET_DK // 2, 1) * sq
        kr = kh * ck + pltpu.roll(kh, RET_DK // 2, 1) * sk
        vb = v[:, hd * RET_DV:(hd + 1) * RET_DV].astype(bf16)
        s = lax.dot_general(qr.astype(bf16), kr.astype(bf16), (((1,), (1,)), ((), ())),
                            preferred_element_type=jnp.float32)
        inner = _dot((s * mask_ref[hd]).astype(bf16), vb)
        qx = (qr * xi_ref[hd]).astype(bf16)
        kz = (kr * zeta_ref[hd]).astype(bf16)
        o_parts = []
        for sq_i in range(n_seq):
            rows = slice(sq_i * seq_rows, (sq_i + 1) * seq_rows)
            r_old = r_scr[sq_i, hd]
            o_parts.append(inner[rows] + _dot(qx[rows], r_old.astype(bf16)))
            kv = lax.dot_general(kz[rows], vb[rows], (((0,), (0,)), ((), ())),
                                 preferred_element_type=jnp.float32)
            r_scr[sq_i, hd] = state_decay[hd] * r_old + kv
        o = o_parts[0] if n_seq == 1 else jnp.concatenate(o_parts, axis=0)
        on = _normalize(o)
        gated = (on * g[:, hd * RET_DV:(hd + 1) * RET_DV]).astype(bf16)
        part = _dot(gated, w_ret_out_ref[hd * RET_DV:(hd + 1) * RET_DV, :])
        y_ret = part if y_ret is None else y_ret + part
        conv_lane_chunk(RET_HEADS + hd)

    for sq_i in range(n_seq):
        for j in range(LANE_CHUNKS):
            ubuf[sq_i, j, 0:HIST_ROWS, :] = ubuf[sq_i, j, seq_rows:seq_rows + HIST_ROWS, :]

    c = _layer_norm(c_scr[...] + conv_b_ref[...], conv_ln_g_ref[...], conv_ln_b_ref[...])
    c = c * _sigmoid(c)
    y_conv = _dot(c.astype(bf16), w_conv_out_ref[...])

    merged =(ga * y_ret + gb * y_conv).astype(bf16)
    h1 = _layer_norm(DN_ALPHA * h + _dot(merged, w_out_ref[...]), ln1_g_ref[...], ln1_b_ref[...])
    _store_row_tiles(h1_ref, h1)

    h1_hi = h1.astype(bf16)
    h1_lo = (h1 - h1_hi.astype(jnp.float32)).astype(bf16)
    both = _dot(h1_hi, w_route_ref[...])
    logits = both[:, :LANES] + both[:, LANES:] + _dot(h1_lo, w_route_hi_ref[...]) + b_route_ref[...]
    lane = lax.broadcasted_iota(jnp.int32, logits.shape, 1)
    lane_f = lane.astype(jnp.float32)
    is_group = lane < N_GROUPS
    gl = jnp.where(is_group, logits, NEG_BIG)
    gmax = jnp.max(gl, axis=-1, keepdims=True)
    gexp = jnp.where(is_group, jnp.exp(gl - gmax), 0.0)
    gprob = gexp / jnp.sum(gexp, axis=-1, keepdims=True)
    gp = jnp.max(gprob, axis=-1, keepdims=True)
    gi = jnp.min(jnp.where(is_group & (gprob == gp), lane_f, float(LANES)), axis=-1, keepdims=True)
    e_lo = N_GROUPS + EXPERTS_PER_GROUP * gi
    in_group = (lane_f >= e_lo) & (lane_f < e_lo + EXPERTS_PER_GROUP)
    el = jnp.where(in_group, logits, NEG_BIG)
    ev1 = jnp.max(el, axis=-1, keepdims=True)
    ei1 = jnp.min(jnp.where(in_group & (el == ev1), lane_f, float(LANES)), axis=-1, keepdims=True)
    rest = in_group & (lane_f != ei1)
    el2 = jnp.where(rest, logits, NEG_BIG)
    ev2 = jnp.max(el2, axis=-1, keepdims=True)
    ei2 = jnp.min(jnp.where(rest & (el2 == ev2), lane_f, float(LANES)), axis=-1, keepdims=True)
    x2 = jnp.exp(ev2 - ev1)
    w1 = gp / (1.0 + x2)
    w2 = w1 * x2
    route = jnp.where(lane == 0, ei1 - N_GROUPS, 0.0)
    route = jnp.where(lane == 1, ei2 - N_GROUPS, route)
    route = jnp.where(lane == 2, w1, route)
    route = jnp.where(lane == 3, w2, route)
    route_ref[...] = route

    @pl.when(t == last_t)
    def _():
        rnew_ref[...] = r_scr[...]
        for j in range(LANE_CHUNKS):
            cnew_ref[:, :, j * LANES:(j + 1) * LANES] = ubuf[:, j, 0:HIST_ROWS, :]


def _const_spec(shape):
    zeros = (0,) * len(shape)
    return pl.BlockSpec(shape, lambda b, t: zeros, pipeline_mode=pl.Buffered(1))


def _mix_call(x, rinit, cinit, tables, weights, shared, *, n_tok_total, n_seq, seq_rows, n_groups, n_steps,
              table_per_step, tile_offset, n_fill_groups):
    tt = n_seq * seq_rows
    assert tt == TOKEN_TILE and x.shape[0] == n_groups * n_steps * tt
    log_decay = [math.log1p(-(2.0 ** (-5.0 - hd))) for hd in range(RET_HEADS)]
    state_decay = tuple(math.exp(lg * seq_rows) for lg in log_decay)
    cq, sq, ck, sk, xi, zeta, mask = tables
    last_group = n_groups - 1
    own = lambda b: jnp.minimum(b, last_group)

    in_tok_spec = pl.BlockSpec((tt, D_MODEL), lambda b, t: (own(b) * n_steps + t, 0))
    out_tok_spec = lambda rows: pl.BlockSpec((rows, LANES), lambda b, t: (tile_offset + b * n_steps + t, 0))
    rope_spec = pl.BlockSpec((tt, LANES), (lambda b, t: (t, 0)) if table_per_step else (lambda b, t: (0, 0)))
    state_r_spec = pl.BlockSpec((n_seq, RET_HEADS, RET_DK, RET_DV), lambda b, t: (own(b), 0, 0, 0))
    state_c_spec = pl.BlockSpec((n_seq, HIST_ROWS, D_MODEL), lambda b, t: (own(b), 0, 0))
    any_spec = pl.BlockSpec(memory_space=pl.ANY)

    in_specs = [in_tok_spec, state_r_spec, state_c_spec, rope_spec, rope_spec, rope_spec, rope_spec,
                _const_spec(xi.shape), _const_spec(zeta.shape), _const_spec(mask.shape)]
    in_specs += [_const_spec(w.shape) for w in weights]
    assert len(in_specs) == N_MIX_INPUTS
    aliases = {}
    extra = ()
    if shared is not None:
        assert n_fill_groups == 0
        in_specs += [any_spec, any_spec]
        aliases = {N_MIX_INPUTS: 0, N_MIX_INPUTS + 1: 1}
        extra = tuple(shared)
    out_shape = (jax.ShapeDtypeStruct((n_tok_total * ROW_TILE_ROWS, LANES), jnp.float32),
                 jax.ShapeDtypeStruct((n_tok_total, LANES), jnp.float32),
                 jax.ShapeDtypeStruct(rinit.shape, jnp.float32),
                 jax.ShapeDtypeStruct(cinit.shape, jnp.float32))
    out_specs = (out_tok_spec(tt * ROW_TILE_ROWS), out_tok_spec(tt), state_r_spec, state_c_spec)
    body = functools.partial(_mix_kernel, n_real_groups=n_groups, n_fill_groups=n_fill_groups,
                             aliased=shared is not None,
                             n_seq=n_seq, seq_rows=seq_rows, state_decay=state_decay)
    return pl.pallas_call(
        body,
        grid=(n_groups + n_fill_groups, n_steps),
        in_specs=in_specs,
        out_specs=out_specs,
        out_shape=out_shape,
        input_output_aliases=aliases,
        scratch_shapes=[pltpu.VMEM((n_seq, RET_HEADS, RET_DK, RET_DV), jnp.float32),
                        pltpu.VMEM((n_seq, LANE_CHUNKS, HIST_ROWS + seq_rows, LANES), jnp.float32),
                        pltpu.VMEM((tt, D_MODEL), jnp.float32)],
        compiler_params=pltpu.CompilerParams(dimension_semantics=("arbitrary", "arbitrary"),
                                             vmem_limit_bytes=VMEM_LIMIT),
        name="mix",
    )(x, rinit, cinit, cq, sq, ck, sk, xi, zeta, mask, *weights, *extra)


def _mix_tables(positions, n_seq, seq_rows):
    half = RET_DK // 2
    inv = ROPE_BASE ** (-np.arange(half, dtype=np.float64) / half)
    ang = np.asarray(positions, np.float64)[:, None] * inv[None, :]
    cos, sin = np.cos(ang), np.sin(ang)
    cq = np.concatenate([cos, cos], axis=1)
    sq = np.concatenate([-sin, sin], axis=1)
    scale = RET_DK ** -0.5
    lg = np.log1p(-np.exp2(-5.0 - np.arange(RET_HEADS, dtype=np.float64)))
    tt = n_seq * seq_rows
    i = np.arange(tt)
    loc = (i % seq_rows).astype(np.float64)
    xi = np.exp(lg[:, None] * (loc + 1.0)[None, :])
    zeta = np.exp(lg[:, None] * (seq_rows - 1.0 - loc)[None, :])
    xi = np.broadcast_to(xi[:, :, None], (RET_HEADS, tt, LANES))
    zeta = np.broadcast_to(zeta[:, :, None], (RET_HEADS, tt, LANES))
    same_seq = (i[:, None] // seq_rows) == (i[None, :] // seq_rows)
    visible = same_seq & ((i[None, :] // CHUNK) <= (i[:, None] // CHUNK))
    dist = np.abs(i[:, None] - i[None, :]).astype(np.float64)
    mask = np.where(visible[None], np.exp(lg[:, None, None] * dist[None]), 0.0)
    tables = (cq, sq, cq * scale, sq * scale, xi, zeta, mask)
    return tuple(jnp.asarray(np.ascontiguousarray(tab), jnp.float32) for tab in tables)


def _expert_kernel(tile_expert_ref, n_used_ref, tile_pair_base_ref, sorted_token_ref, h1_hbm, wg_ref, wu_ref,
                   wd_ref, ys_ref, xbuf, sems, wg_b, wu_b, wd_b):
    i = pl.program_id(0)
    n_used = n_used_ref[0]
    bf16 = jnp.bfloat16
    slot = lax.rem(i, GATHER_SLOTS)

    def start_gather(tile):
        dst_slot = lax.rem(tile, GATHER_SLOTS)
        pair_base = tile_pair_base_ref[tile]

        def issue(r, carry):
            src = pl.multiple_of(sorted_token_ref[pair_base + r] * ROW_TILE_ROWS, ROW_TILE_ROWS)
            dst = pl.multiple_of(r * ROW_TILE_ROWS, ROW_TILE_ROWS)
            pltpu.make_async_copy(h1_hbm.at[pl.ds(src, ROW_TILE_ROWS)],
                                  xbuf.at[dst_slot, pl.ds(dst, ROW_TILE_ROWS)],
                                  sems.at[dst_slot]).start(priority=1)
            return carry
        lax.fori_loop(0, EXPERT_TILE, issue, 0, unroll=8)

    def wait_gather(dst_slot):
        pltpu.make_async_copy(h1_hbm.at[pl.ds(0, EXPERT_TILE * ROW_TILE_ROWS)], xbuf.at[dst_slot],
                              sems.at[dst_slot]).wait()

    for first in range(GATHER_SLOTS - 1):
        @pl.when((i == 0) & (first < n_used))
        def _(first=first):
            start_gather(first)

    @pl.when(i + GATHER_SLOTS - 1 < n_used)
    def _():
        start_gather(i + GATHER_SLOTS - 1)

    prev = tile_expert_ref[jnp.maximum(i - 1, 0)]
    new_expert = (i == 0) | (tile_expert_ref[i] != prev)

    @pl.when(new_expert)
    def _():
        wg_b[...] = wg_ref[0].astype(bf16)
        wu_b[...] = wu_ref[0].astype(bf16)
        wd_b[...] = wd_ref[0].astype(bf16)

    @pl.when(i < n_used)
    def _():
        wait_gather(slot)
        x = _load_row_tiles(xbuf.at[slot], EXPERT_TILE).astype(bf16)
        gate = _dot(x, wg_b[...])
        up = _dot(x, wu_b[...])
        act = (gate * _sigmoid(gate) * up).astype(bf16)
        _store_row_tiles(ys_ref, _dot(act, wd_b[...]))

    @pl.when(i >= n_used)
    def _():
        ys_ref[...] = jnp.zeros_like(ys_ref)


def _expert_call(tile_expert, n_used, tile_pair_base, sorted_token, h1, wg, wu, wd):
    n_tiles = tile_expert.shape[0]
    w_map = lambda i, te, nu, pb, st: (te[i], 0, 0)
    grid_spec = pltpu.PrefetchScalarGridSpec(
        num_scalar_prefetch=4,
        grid=(n_tiles,),
        in_specs=[pl.BlockSpec(memory_space=pl.ANY),
                  pl.BlockSpec((1, D_MODEL, EXPERT_FF), w_map),
                  pl.BlockSpec((1, D_MODEL, EXPERT_FF), w_map),
                  pl.BlockSpec((1, EXPERT_FF, D_MODEL), w_map)],
        out_specs=pl.BlockSpec((EXPERT_TILE * ROW_TILE_ROWS, LANES), lambda i, te, nu, pb, st: (i, 0)),
        scratch_shapes=[pltpu.VMEM((GATHER_SLOTS, EXPERT_TILE * ROW_TILE_ROWS, LANES), jnp.float32),
                        pltpu.SemaphoreType.DMA((GATHER_SLOTS,)),
                        pltpu.VMEM((D_MODEL, EXPERT_FF), jnp.bfloat16),
                        pltpu.VMEM((D_MODEL, EXPERT_FF), jnp.bfloat16),
                        pltpu.VMEM((EXPERT_FF, D_MODEL), jnp.bfloat16)],
    )
    return pl.pallas_call(
        _expert_kernel,
        grid_spec=grid_spec,
        out_shape=jax.ShapeDtypeStruct((n_tiles * EXPERT_TILE * ROW_TILE_ROWS, LANES), jnp.float32),
        compiler_params=pltpu.CompilerParams(dimension_semantics=("arbitrary",),
                                             vmem_limit_bytes=VMEM_LIMIT),
        name="experts",
    )(tile_expert, n_used, tile_pair_base, sorted_token, h1, wg, wu, wd)


def _combine_kernel(pos_ref, h1_ref, route_ref, pp_ref, ps_ref, ys_hbm, w_ple_proj_ref, w_ple_gate_ref,
                    ln2_g_ref, ln2_b_ref, yp_ref, ys_out_ref, ybuf, sems, *, n_prompt_tiles):
    i = pl.program_id(0)
    n_steps = pl.num_programs(0)
    bf16 = jnp.bfloat16
    slot = lax.rem(i, 2)

    def start_gather(tile, dst_slot):
        def issue(r, carry):
            pair = 2 * (tile * TOKEN_TILE + r)
            dst = pl.multiple_of(r * ROW_TILE_ROWS, ROW_TILE_ROWS)
            for which in range(2):
                src = pl.multiple_of(pos_ref[pair + which] * ROW_TILE_ROWS, ROW_TILE_ROWS)
                pltpu.make_async_copy(ys_hbm.at[pl.ds(src, ROW_TILE_ROWS)],
                                      ybuf.at[dst_slot, which, pl.ds(dst, ROW_TILE_ROWS)],
                                      sems.at[dst_slot]).start(priority=which)
            return carry
        lax.fori_loop(0, TOKEN_TILE, issue, 0, unroll=4)

    def wait_gather(dst_slot):
        for which in range(2):
            pltpu.make_async_copy(ys_hbm.at[pl.ds(0, TOKEN_TILE * ROW_TILE_ROWS)], ybuf.at[dst_slot, which],
                                  sems.at[dst_slot]).wait()

    @pl.when(i == 0)
    def _():
        start_gather(0, 0)

    @pl.when(i + 1 < n_steps)
    def _():
        start_gather(i + 1, 1 - slot)

    h1 = _load_row_tiles(h1_ref, TOKEN_TILE)
    is_prompt = i < n_prompt_tiles
    p = jnp.where(is_prompt, pp_ref[...], ps_ref[...])
    pe = _dot(p.astype(bf16), w_ple_proj_ref[...]) * _sigmoid(_dot(h1.astype(bf16), w_ple_gate_ref[...]))
    route = route_ref[...]
    w1 = route[:, 2:3]
    w2 = route[:, 3:4]
    wait_gather(slot)
    y1 = _load_row_tiles(ybuf.at[slot, 0], TOKEN_TILE)
    y2 = _load_row_tiles(ybuf.at[slot, 1], TOKEN_TILE)
    total = DN_ALPHA * h1 + pe + w1 * y1 + w2 * y2
    out = _layer_norm(total, ln2_g_ref[...], ln2_b_ref[...])

    @pl.when(is_prompt)
    def _():
        yp_ref[...] = out

    @pl.when(jnp.logical_not(is_prompt))
    def _():
        ys_out_ref[...] = out


def _combine_call(pos, h1, route, p_prompt, p_sample, ys, w_ple_proj, w_ple_gate, ln2_g, ln2_b):
    n_tok = route.shape[0]
    n_prompt_tiles = p_prompt.shape[0] // TOKEN_TILE
    n_sample_tiles = p_sample.shape[0] // TOKEN_TILE
    n_steps = n_tok // TOKEN_TILE
    assert n_steps == n_prompt_tiles + n_sample_tiles
    tok_spec = lambda cols: pl.BlockSpec((TOKEN_TILE, cols), lambda i, pos: (i, 0))
    prompt_spec = lambda cols: pl.BlockSpec((TOKEN_TILE, cols),
                                            lambda i, pos: (jnp.minimum(i, n_prompt_tiles - 1), 0))
    sample_spec = lambda cols: pl.BlockSpec((TOKEN_TILE, cols),
                                            lambda i, pos: (jnp.maximum(i - n_prompt_tiles, 0), 0))
    const_spec = lambda shape: pl.BlockSpec(shape, lambda i, pos: (0, 0))
    grid_spec = pltpu.PrefetchScalarGridSpec(
        num_scalar_prefetch=1,
        grid=(n_steps,),
        in_specs=[pl.BlockSpec((TOKEN_TILE * ROW_TILE_ROWS, LANES), lambda i, pos: (i, 0)), tok_spec(LANES),
                  prompt_spec(PLE_DIM), sample_spec(PLE_DIM),
                  pl.BlockSpec(memory_space=pl.ANY), const_spec(w_ple_proj.shape), const_spec(w_ple_gate.shape),
                  const_spec(ln2_g.shape), const_spec(ln2_b.shape)],
        out_specs=(prompt_spec(D_MODEL), sample_spec(D_MODEL)),
        scratch_shapes=[pltpu.VMEM((2, 2, TOKEN_TILE * ROW_TILE_ROWS, LANES), jnp.float32),
                        pltpu.SemaphoreType.DMA((2,))],
    )
    return pl.pallas_call(
        functools.partial(_combine_kernel, n_prompt_tiles=n_prompt_tiles),
        grid_spec=grid_spec,
        out_shape=(jax.ShapeDtypeStruct((n_prompt_tiles * TOKEN_TILE, D_MODEL), jnp.float32),
                   jax.ShapeDtypeStruct((n_sample_tiles * TOKEN_TILE, D_MODEL), jnp.float32)),
        compiler_params=pltpu.CompilerParams(dimension_semantics=("arbitrary",),
                                             vmem_limit_bytes=VMEM_LIMIT),
        name="combine",
    )(pos, h1, route, p_prompt, p_sample, ys, w_ple_proj, w_ple_gate, ln2_g, ln2_b)


def _dispatch_plan(route, n_tiles):
    f32 = jnp.float32
    exact = lax.Precision.HIGHEST
    experts = route[:, :2].astype(jnp.int32).reshape(-1)
    n_pairs = experts.shape[0]
    expert_ids = jnp.arange(N_EXPERTS, dtype=jnp.int32)
    onehot = (experts[None, :] == expert_ids[:, None]).astype(jnp.int32)
    running = jnp.cumsum(onehot, axis=1)
    counts = running[:, -1].astype(f32)
    tiles_per = jnp.floor((counts + (EXPERT_TILE - 1)) / EXPERT_TILE)

    upper = (expert_ids[:, None] <= expert_ids[None, :]).astype(f32)
    ends = jnp.dot(jnp.stack([tiles_per, counts]), upper, precision=exact)
    tile_end, pair_end = ends[0], ends[1]
    tile_start = tile_end - tiles_per
    row_start = tile_start * EXPERT_TILE
    pos = jnp.sum(onehot * (running - 1 + row_start.astype(jnp.int32)[:, None]), axis=0)
    n_used = tile_end[-1]
    tile_index = jnp.arange(n_tiles, dtype=jnp.int32).astype(f32)
    clamped = jnp.minimum(tile_index, n_used - 1)[:, None]
    member = ((clamped >= tile_start[None, :]) & (clamped < tile_end[None, :])).astype(f32)
    table = jnp.stack([expert_ids.astype(f32), pair_end - counts - row_start], axis=1)
    per_tile = jnp.dot(member, table, precision=exact)
    tile_expert = per_tile[:, 0]
    tile_pair_base = jnp.clip(per_tile[:, 1] + tile_index * EXPERT_TILE, 0, n_pairs)

    assert n_pairs <= 1 << PAIR_BITS
    keys = jnp.sort(experts * (1 << PAIR_BITS) + jnp.arange(n_pairs, dtype=jnp.int32))
    sorted_token = jnp.concatenate([(keys & ((1 << PAIR_BITS) - 1)) >> 1,
                                    jnp.zeros((EXPERT_TILE,), jnp.int32)])
    return (pos.astype(jnp.int32), sorted_token, tile_expert.astype(jnp.int32),
            tile_pair_base.astype(jnp.int32), n_used.reshape(1).astype(jnp.int32))


def kernel(x_prompt, x_sample, p_prompt, p_sample, state_ret, state_conv, ln_in_g, ln_in_b, w_in, w_ret_out,
           conv_w, conv_b, conv_ln_g, conv_ln_b, w_conv_out, w_out, ln1_g, ln1_b, w_route_g, b_route_g,
           w_route_e, b_route_e, w_exp_gate, w_exp_up, w_exp_down, w_ple_proj, w_ple_gate, ln2_g, ln2_b):
    assert w_in.shape[0] == DEPTH == 1
    bf16 = jnp.bfloat16
    batch, seq, _ = x_prompt.shape
    dec_batch, dec_seq, _ = x_sample.shape
    assert dec_seq == CHUNK and seq % TOKEN_TILE == 0 and TOKEN_TILE % CHUNK == 0
    row = lambda a: a.reshape(1, -1)
    n_prompt = batch * seq
    n_sample = dec_batch * dec_seq
    n_tok = n_prompt + n_sample

    w_route = jnp.concatenate(
        [w_route_g[0], jnp.transpose(w_route_e[0], (1, 0, 2)).reshape(D_MODEL, N_EXPERTS),
         jnp.zeros((D_MODEL, LANES - ROUTE_COLS), jnp.float32)], axis=1)
    w_route_hi = w_route.astype(bf16)
    w_route_lo = (w_route - w_route_hi.astype(jnp.float32)).astype(bf16)
    b_route = jnp.concatenate([b_route_g[0], b_route_e[0].reshape(-1),
                               jnp.zeros((LANES - ROUTE_COLS,), jnp.float32)]).reshape(1, LANES)
    conv_w_pad = jnp.concatenate([conv_w[0], jnp.zeros((HIST_ROWS - CONV_WIDTH, D_MODEL), jnp.float32)], axis=0)
    weights = (row(ln_in_g), row(ln_in_b), w_in[0].astype(bf16), w_ret_out[0].astype(bf16), conv_w_pad,
               row(conv_b[0]), row(conv_ln_g[0]), row(conv_ln_b[0]), w_conv_out[0].astype(bf16),
               w_out[0].astype(bf16), row(ln1_g[0]), row(ln1_b[0]),
               jnp.concatenate([w_route_hi, w_route_lo], axis=1), w_route_hi, b_route)

    steps_p = seq // TOKEN_TILE
    sample_tiles = n_sample // TOKEN_TILE
    assert sample_tiles % steps_p == 0
    tables_p = _mix_tables(np.arange(seq), 1, TOKEN_TILE)
    rinit_p = jnp.zeros((batch, RET_HEADS, RET_DK, RET_DV), jnp.float32)
    cinit_p = jnp.zeros((batch, HIST_ROWS, D_MODEL), jnp.float32)
    h1, route, rnew_p, cnew_p = _mix_call(
        x_prompt.reshape(n_prompt, D_MODEL), rinit_p, cinit_p, tables_p, weights, None, n_tok_total=n_tok,
        n_seq=1, seq_rows=TOKEN_TILE, n_groups=batch, n_steps=steps_p, table_per_step=True,
        tile_offset=0, n_fill_groups=sample_tiles // steps_p)

    seqs_per_tile = TOKEN_TILE // dec_seq
    assert dec_batch % seqs_per_tile == 0
    pos_s = np.tile(PAST_LEN + np.arange(dec_seq), seqs_per_tile)
    tables_s = _mix_tables(pos_s, seqs_per_tile, dec_seq)
    cinit_s = jnp.pad(state_conv[0], ((0, 0), (HIST_PAD, 0), (0, 0)))
    h1, route, rnew_s, cnew_s = _mix_call(
        x_sample.reshape(n_sample, D_MODEL), state_ret[0], cinit_s, tables_s, weights, (h1, route),
        n_tok_total=n_tok, n_seq=seqs_per_tile, seq_rows=dec_seq, n_groups=dec_batch // seqs_per_tile,
        n_steps=1, table_per_step=False, tile_offset=n_prompt // TOKEN_TILE, n_fill_groups=0)

    n_tiles = (2 * n_tok) // EXPERT_TILE + N_EXPERTS
    pos, sorted_token, tile_expert, tile_pair_base, n_used = _dispatch_plan(route, n_tiles)
    ys = _expert_call(tile_expert, n_used, tile_pair_base, sorted_token, h1,
                      w_exp_gate[0].reshape(N_EXPERTS, D_MODEL, EXPERT_FF),
                      w_exp_up[0].reshape(N_EXPERTS, D_MODEL, EXPERT_FF),
                      w_exp_down[0].reshape(N_EXPERTS, EXPERT_FF, D_MODEL))
    y_p, y_s = _combine_call(pos, h1, route, p_prompt[0].reshape(n_prompt, PLE_DIM),
                             p_sample[0].reshape(n_sample, PLE_DIM), ys, w_ple_proj[0].astype(bf16),
                             w_ple_gate[0].astype(bf16), row(ln2_g[0]), row(ln2_b[0]))

    return (y_p.reshape(batch, seq, D_MODEL), y_s.reshape(dec_batch, dec_seq, D_MODEL), rnew_p[None],
            cnew_p[None, :, HIST_PAD:, :], rnew_s[None], cnew_s[None, :, HIST_PAD:, :])
```

```python
import functools
import math

import jax
import jax.numpy as jnp
import numpy as np
from jax import lax
from jax.experimental import pallas as pl
from jax.experimental.pallas import tpu as pltpu

D_MODEL = 1024
RET_HEADS = 4
RET_DK = 128
RET_DV = 256
RET_QK = RET_HEADS * RET_DK
RET_V = RET_HEADS * RET_DV
CHUNK = 64
CONV_WIDTH = 31
CONV_HIST = CONV_WIDTH - 1
HIST_ROWS = 32
HIST_PAD = HIST_ROWS - CONV_HIST
N_GROUPS = 4
EXPERTS_PER_GROUP = 8
N_EXPERTS = N_GROUPS * EXPERTS_PER_GROUP
EXPERT_FF = 512
PLE_DIM = 256
PAST_LEN = 4096
LN_EPS = 1e-5
ROPE_BASE = 10000.0
DEPTH = 1
DN_ALPHA = float((2 * DEPTH) ** 0.25)

LANES = 128
TOKEN_TILE = 256
EXPERT_TILE = 256
ROUTE_TILE = 2048
GATHER_SLOTS = 3
LANE_CHUNKS = D_MODEL // LANES
ROW_TILE_ROWS = D_MODEL // LANES
CONV_ROW_BLOCK = 64
VMEM_LIMIT = 56 * 1024 * 1024

C_Q = 0
C_K = RET_QK
C_V = 2 * RET_QK
C_G = C_V + RET_V
C_GLU_A = C_G + RET_V
C_GLU_B = C_GLU_A + D_MODEL
C_GA = C_GLU_B + D_MODEL
C_GB = C_GA + D_MODEL

ROUTE_COLS = N_GROUPS + N_EXPERTS
PAIR_BITS = 16
NEG_BIG = -1e30
NEG_LOG2_E = -math.log2(math.e)


def _sigmoid(x):
    return 1.0 / (1.0 + jnp.exp2(x * NEG_LOG2_E))


def _normalize(x):
    mu = jnp.mean(x, axis=-1, keepdims=True)
    xc = x - mu
    var = jnp.mean(xc * xc, axis=-1, keepdims=True)
    return xc * lax.rsqrt(var + LN_EPS)


def _layer_norm(x, g, b):
    return _normalize(x) * g + b


def _dot(a, b):
    return jnp.dot(a, b, preferred_element_type=jnp.float32)


def _store_row_tiles(ref, x):
    n = x.shape[0]
    for j in range(ROW_TILE_ROWS):
        ref[pl.ds(j, n, stride=ROW_TILE_ROWS), :] = x[:, j * LANES:(j + 1) * LANES]


def _load_row_tiles(ref, n):
    return jnp.concatenate([ref[pl.ds(j, n, stride=ROW_TILE_ROWS), :] for j in range(ROW_TILE_ROWS)], axis=1)


N_MIX_INPUTS = 25


def _mix_kernel(*refs, n_real_groups, n_fill_groups, aliased, **tile_params):
    if aliased:
        refs = refs[:N_MIX_INPUTS] + refs[N_MIX_INPUTS + 2:]
    h1_ref, route_ref = refs[N_MIX_INPUTS], refs[N_MIX_INPUTS + 1]
    if n_fill_groups == 0:
        _mix_tile(*refs, **tile_params)
        return
    group = pl.program_id(0)

    @pl.when(group < n_real_groups)
    def _():
        _mix_tile(*refs, **tile_params)

    @pl.when(group >= n_real_groups)
    def _():
        h1_ref[...] = jnp.zeros_like(h1_ref)
        route_ref[...] = jnp.zeros_like(route_ref)


def _mix_tile(x_ref, rinit_ref, cinit_ref, cq_ref, sq_ref, ck_ref, sk_ref,
              xi_ref, zeta_ref, mask_ref,
              ln_in_g_ref, ln_in_b_ref, w_in_ref, w_ret_out_ref, conv_w_ref, conv_b_ref,
              conv_ln_g_ref, conv_ln_b_ref, w_conv_out_ref, w_out_ref, ln1_g_ref, ln1_b_ref,
              w_route_ref, w_route_hi_ref, b_route_ref,
              h1_ref, route_ref, rnew_ref, cnew_ref,
              r_scr, ubuf, c_scr,
              *, n_seq, seq_rows, state_decay):
    t = pl.program_id(1)
    last_t = pl.num_programs(1) - 1
    bf16 = jnp.bfloat16

    @pl.when(t == 0)
    def _():
        r_scr[...] = rinit_ref[...]
        for j in range(LANE_CHUNKS):
            ubuf[:, j, 0:HIST_ROWS, :] = cinit_ref[:, :, j * LANES:(j + 1) * LANES]

    h = _layer_norm(x_ref[...], ln_in_g_ref[...], ln_in_b_ref[...])
    hb = h.astype(bf16)

    u = _dot(hb, w_in_ref[:, C_GLU_A:C_GLU_A + D_MODEL]) * _sigmoid(
        _dot(hb, w_in_ref[:, C_GLU_B:C_GLU_B + D_MODEL]))
    for sq_i in range(n_seq):
        for j in range(LANE_CHUNKS):
            ubuf[sq_i, j, HIST_ROWS:HIST_ROWS + seq_rows, :] = u[sq_i * seq_rows:(sq_i + 1) * seq_rows,
                                                                 j * LANES:(j + 1) * LANES]

    row_block = min(CONV_ROW_BLOCK, seq_rows)

    def conv_lane_chunk(j):
        lane = slice(j * LANES, (j + 1) * LANES)
        for sq_i in range(n_seq):
            for rb in range(seq_rows // row_block):
                r0 = rb * row_block + HIST_PAD
                acc = None
                for kk in range(CONV_WIDTH):
                    term = conv_w_ref[kk:kk + 1, lane] * ubuf[sq_i, j, r0 + kk:r0 + kk + row_block, :]
                    acc = term if acc is None else acc + term
                out_row = sq_i * seq_rows + rb * row_block
                c_scr[out_row:out_row + row_block, lane] = acc

    assert LANE_CHUNKS == 2 * RET_HEADS

    q = _dot(hb, w_in_ref[:, C_Q:C_Q + RET_QK])
    conv_lane_chunk(0)
    k = _dot(hb, w_in_ref[:, C_K:C_K + RET_QK])
    conv_lane_chunk(1)
    v = _dot(hb, w_in_ref[:, C_V:C_V + RET_V])
    conv_lane_chunk(2)
    g = _dot(hb, w_in_ref[:, C_G:C_G + RET_V])
    g = g * _sigmoid(g)
    ga = _sigmoid(_dot(hb, w_in_ref[:, C_GA:C_GA + D_MODEL]))
    conv_lane_chunk(3)
    gb = _sigmoid(_dot(hb, w_in_ref[:, C_GB:C_GB + D_MODEL]))
    cq, sq, ck, sk = cq_ref[...], sq_ref[...], ck_ref[...], sk_ref[...]

    y_ret = None
    for hd in range(RET_HEADS):
        qh = q[:, hd * RET_DK:(hd + 1) * RET_DK]
        kh = k[:, hd * RET_DK:(hd + 1) * RET_DK]
        qr = qh * cq + pltpu.roll(qh, RET_DK // 2, 1) * sq
        kr = kh * ck + pltpu.roll(kh, RET_DK // 2, 1) * sk
        vb = v[:, hd * RET_DV:(hd + 1) * RET_DV].astype(bf16)
        s = lax.dot_general(qr.astype(bf16), kr.astype(bf16), (((1,), (1,)), ((), ())),
                            preferred_element_type=jnp.float32)
        inner = _dot((s * mask_ref[hd]).astype(bf16), vb)
        qx = (qr * xi_ref[hd]).astype(bf16)
        kz = (kr * zeta_ref[hd]).astype(bf16)
        o_parts = []
        for sq_i in range(n_seq):
            rows = slice(sq_i * seq_rows, (sq_i + 1) * seq_rows)
            r_old = r_scr[sq_i, hd]
            o_parts.append(inner[rows] + _dot(qx[rows], r_old.astype(bf16)))
            kv = lax.dot_general(kz[rows], vb[rows], (((0,), (0,)), ((), ())),
                                 preferred_element_type=jnp.float32)
            r_scr[sq_i, hd] = state_decay[hd] * r_old + kv
        o = o_parts[0] if n_seq == 1 else jnp.concatenate(o_parts, axis=0)
        on = _normalize(o)
        gated = (on * g[:, hd * RET_DV:(hd + 1) * RET_DV]).astype(bf16)
        part = _dot(gated, w_ret_out_ref[hd * RET_DV:(hd + 1) * RET_DV, :])
        y_ret = part if y_ret is None else y_ret + part
        conv_lane_chunk(RET_HEADS + hd)

    for sq_i in range(n_seq):
        for j in range(LANE_CHUNKS):
            ubuf[sq_i, j, 0:HIST_ROWS, :] = ubuf[sq_i, j, seq_rows:seq_rows + HIST_ROWS, :]

    c = _layer_norm(c_scr[...] + conv_b_ref[...], conv_ln_g_ref[...], conv_ln_b_ref[...])
    c = c * _sigmoid(c)
    y_conv = _dot(c.astype(bf16), w_conv_out_ref[...])

    merged =(ga * y_ret + gb * y_conv).astype(bf16)
    h1 = _layer_norm(DN_ALPHA * h + _dot(merged, w_out_ref[...]), ln1_g_ref[...], ln1_b_ref[...])
    _store_row_tiles(h1_ref, h1)

    h1_hi = h1.astype(bf16)
    h1_lo = (h1 - h1_hi.astype(jnp.float32)).astype(bf16)
    both = _dot(h1_hi, w_route_ref[...])
    route_ref[...] = both[:, :LANES] + both[:, LANES:] + _dot(h1_lo, w_route_hi_ref[...]) + b_route_ref[...]

    @pl.when(t == last_t)
    def _():
        rnew_ref[...] = r_scr[...]
        for j in range(LANE_CHUNKS):
            cnew_ref[:, :, j * LANES:(j + 1) * LANES] = ubuf[:, j, 0:HIST_ROWS, :]


def _route_kernel(logits_ref, route_ref):
    logits = logits_ref[...]
    lane = lax.broadcasted_iota(jnp.int32, logits.shape, 1)
    lane_f = lane.astype(jnp.float32)
    is_group = lane < N_GROUPS
    gl = jnp.where(is_group, logits, NEG_BIG)
    gmax = jnp.max(gl, axis=-1, keepdims=True)
    gexp = jnp.where(is_group, jnp.exp(gl - gmax), 0.0)
    gprob = gexp / jnp.sum(gexp, axis=-1, keepdims=True)
    gp = jnp.max(gprob, axis=-1, keepdims=True)
    gi = jnp.min(jnp.where(is_group & (gprob == gp), lane_f, float(LANES)), axis=-1, keepdims=True)
    e_lo = N_GROUPS + EXPERTS_PER_GROUP * gi
    in_group = (lane_f >= e_lo) & (lane_f < e_lo + EXPERTS_PER_GROUP)
    el = jnp.where(in_group, logits, NEG_BIG)
    ev1 = jnp.max(el, axis=-1, keepdims=True)
    ei1 = jnp.min(jnp.where(in_group & (el == ev1), lane_f, float(LANES)), axis=-1, keepdims=True)
    rest = in_group & (lane_f != ei1)
    el2 = jnp.where(rest, logits, NEG_BIG)
    ev2 = jnp.max(el2, axis=-1, keepdims=True)
    ei2 = jnp.min(jnp.where(rest & (el2 == ev2), lane_f, float(LANES)), axis=-1, keepdims=True)
    x2 = jnp.exp(ev2 - ev1)
    w1 = gp / (1.0 + x2)
    w2 = w1 * x2
    route = jnp.where(lane == 0, ei1 - N_GROUPS, 0.0)
    route = jnp.where(lane == 1, ei2 - N_GROUPS, route)
    route = jnp.where(lane == 2, w1, route)
    route = jnp.where(lane == 3, w2, route)
    route_ref[...] = route


def _route_call(logits):
    n_tok = logits.shape[0]
    assert n_tok % ROUTE_TILE == 0
    spec = pl.BlockSpec((ROUTE_TILE, LANES), lambda i: (i, 0))
    return pl.pallas_call(
        _route_kernel,
        grid=(n_tok // ROUTE_TILE,),
        in_specs=[spec],
        out_specs=spec,
        out_shape=jax.ShapeDtypeStruct(logits.shape, jnp.float32),
        compiler_params=pltpu.CompilerParams(dimension_semantics=("arbitrary",)),
        name="route",
    )(logits)


def _const_spec(shape):
    zeros = (0,) * len(shape)
    return pl.BlockSpec(shape, lambda b, t: zeros, pipeline_mode=pl.Buffered(1))


def _mix_call(x, rinit, cinit, tables, weights, shared, *, n_tok_total, n_seq, seq_rows, n_groups, n_steps,
              table_per_step, tile_offset, n_fill_groups):
    tt = n_seq * seq_rows
    assert tt == TOKEN_TILE and x.shape[0] == n_groups * n_steps * tt
    log_decay = [math.log1p(-(2.0 ** (-5.0 - hd))) for hd in range(RET_HEADS)]
    state_decay = tuple(math.exp(lg * seq_rows) for lg in log_decay)
    cq, sq, ck, sk, xi, zeta, mask = tables
    last_group = n_groups - 1
    own = lambda b: jnp.minimum(b, last_group)

    in_tok_spec = pl.BlockSpec((tt, D_MODEL), lambda b, t: (own(b) * n_steps + t, 0))
    out_tok_spec = lambda rows: pl.BlockSpec((rows, LANES), lambda b, t: (tile_offset + b * n_steps + t, 0))
    rope_spec = pl.BlockSpec((tt, LANES), (lambda b, t: (t, 0)) if table_per_step else (lambda b, t: (0, 0)))
    state_r_spec = pl.BlockSpec((n_seq, RET_HEADS, RET_DK, RET_DV), lambda b, t: (own(b), 0, 0, 0))
    state_c_spec = pl.BlockSpec((n_seq, HIST_ROWS, D_MODEL), lambda b, t: (own(b), 0, 0))
    any_spec = pl.BlockSpec(memory_space=pl.ANY)

    in_specs = [in_tok_spec, state_r_spec, state_c_spec, rope_spec, rope_spec, rope_spec, rope_spec,
                _const_spec(xi.shape), _const_spec(zeta.shape), _const_spec(mask.shape)]
    in_specs += [_const_spec(w.shape) for w in weights]
    assert len(in_specs) == N_MIX_INPUTS
    aliases = {}
    extra = ()
    if shared is not None:
        assert n_fill_groups == 0
        in_specs += [any_spec, any_spec]
        aliases = {N_MIX_INPUTS: 0, N_MIX_INPUTS + 1: 1}
        extra = tuple(shared)
    out_shape = (jax.ShapeDtypeStruct((n_tok_total * ROW_TILE_ROWS, LANES), jnp.float32),
                 jax.ShapeDtypeStruct((n_tok_total, LANES), jnp.float32),
                 jax.ShapeDtypeStruct(rinit.shape, jnp.float32),
                 jax.ShapeDtypeStruct(cinit.shape, jnp.float32))
    out_specs = (out_tok_spec(tt * ROW_TILE_ROWS), out_tok_spec(tt), state_r_spec, state_c_spec)
    body = functools.partial(_mix_kernel, n_real_groups=n_groups, n_fill_groups=n_fill_groups,
                             aliased=shared is not None,
                             n_seq=n_seq, seq_rows=seq_rows, state_decay=state_decay)
    return pl.pallas_call(
        body,
        grid=(n_groups + n_fill_groups, n_steps),
        in_specs=in_specs,
        out_specs=out_specs,
        out_shape=out_shape,
        input_output_aliases=aliases,
        scratch_shapes=[pltpu.VMEM((n_seq, RET_HEADS, RET_DK, RET_DV), jnp.float32),
                        pltpu.VMEM((n_seq, LANE_CHUNKS, HIST_ROWS + seq_rows, LANES), jnp.float32),
                        pltpu.VMEM((tt, D_MODEL), jnp.float32)],
        compiler_params=pltpu.CompilerParams(dimension_semantics=("arbitrary", "arbitrary"),
                                             vmem_limit_bytes=VMEM_LIMIT),
        name="mix",
    )(x, rinit, cinit, cq, sq, ck, sk, xi, zeta, mask, *weights, *extra)


def _mix_tables(positions, n_seq, seq_rows):
    half = RET_DK // 2
    inv = ROPE_BASE ** (-np.arange(half, dtype=np.float64) / half)
    ang = np.asarray(positions, np.float64)[:, None] * inv[None, :]
    cos, sin = np.cos(ang), np.sin(ang)
    cq = np.concatenate([cos, cos], axis=1)
    sq = np.concatenate([-sin, sin], axis=1)
    scale = RET_DK ** -0.5
    lg = np.log1p(-np.exp2(-5.0 - np.arange(RET_HEADS, dtype=np.float64)))
    tt = n_seq * seq_rows
    i = np.arange(tt)
    loc = (i % seq_rows).astype(np.float64)
    xi = np.exp(lg[:, None] * (loc + 1.0)[None, :])
    zeta = np.exp(lg[:, None] * (seq_rows - 1.0 - loc)[None, :])
    xi = np.broadcast_to(xi[:, :, None], (RET_HEADS, tt, LANES))
    zeta = np.broadcast_to(zeta[:, :, None], (RET_HEADS, tt, LANES))
    same_seq = (i[:, None] // seq_rows) == (i[None, :] // seq_rows)
    visible = same_seq & ((i[None, :] // CHUNK) <= (i[:, None] // CHUNK))
    dist = np.abs(i[:, None] - i[None, :]).astype(np.float64)
    mask = np.where(visible[None], np.exp(lg[:, None, None] * dist[None]), 0.0)
    tables = (cq, sq, cq * scale, sq * scale, xi, zeta, mask)
    return tuple(jnp.asarray(np.ascontiguousarray(tab), jnp.float32) for tab in tables)


def _expert_kernel(tile_expert_ref, n_used_ref, tile_pair_base_ref, sorted_token_ref, h1_hbm, wg_ref, wu_ref,
                   wd_ref, ys_ref, xbuf, sems, wg_b, wu_b, wd_b):
    i = pl.program_id(0)
    n_used = n_used_ref[0]
    bf16 = jnp.bfloat16
    slot = lax.rem(i, GATHER_SLOTS)

    def start_gather(tile):
        dst_slot = lax.rem(tile, GATHER_SLOTS)
        pair_base = tile_pair_base_ref[tile]

        def issue(r, carry):
            src = pl.multiple_of(sorted_token_ref[pair_base + r] * ROW_TILE_ROWS, ROW_TILE_ROWS)
            dst = pl.multiple_of(r * ROW_TILE_ROWS, ROW_TILE_ROWS)
            pltpu.make_async_copy(h1_hbm.at[pl.ds(src, ROW_TILE_ROWS)],
                                  xbuf.at[dst_slot, pl.ds(dst, ROW_TILE_ROWS)],
                                  sems.at[dst_slot]).start(priority=1)
            return carry
        lax.fori_loop(0, EXPERT_TILE, issue, 0, unroll=8)

    def wait_gather(dst_slot):
        pltpu.make_async_copy(h1_hbm.at[pl.ds(0, EXPERT_TILE * ROW_TILE_ROWS)], xbuf.at[dst_slot],
                              sems.at[dst_slot]).wait()

    for first in range(GATHER_SLOTS - 1):
        @pl.when((i == 0) & (first < n_used))
        def _(first=first):
            start_gather(first)

    @pl.when(i + GATHER_SLOTS - 1 < n_used)
    def _():
        start_gather(i + GATHER_SLOTS - 1)

    prev = tile_expert_ref[jnp.maximum(i - 1, 0)]
    new_expert = (i == 0) | (tile_expert_ref[i] != prev)

    @pl.when(new_expert)
    def _():
        wg_b[...] = wg_ref[0].astype(bf16)
        wu_b[...] = wu_ref[0].astype(bf16)
        wd_b[...] = wd_ref[0].astype(bf16)

    @pl.when(i < n_used)
    def _():
        wait_gather(slot)
        x = _load_row_tiles(xbuf.at[slot], EXPERT_TILE).astype(bf16)
        gate = _dot(x, wg_b[...])
        up = _dot(x, wu_b[...])
        act = (gate * _sigmoid(gate) * up).astype(bf16)
        _store_row_tiles(ys_ref, _dot(act, wd_b[...]))

    @pl.when(i >= n_used)
    def _():
        ys_ref[...] = jnp.zeros_like(ys_ref)


def _expert_call(tile_expert, n_used, tile_pair_base, sorted_token, h1, wg, wu, wd):
    n_tiles = tile_expert.shape[0]
    w_map = lambda i, te, nu, pb, st: (te[i], 0, 0)
    grid_spec = pltpu.PrefetchScalarGridSpec(
        num_scalar_prefetch=4,
        grid=(n_tiles,),
        in_specs=[pl.BlockSpec(memory_space=pl.ANY),
                  pl.BlockSpec((1, D_MODEL, EXPERT_FF), w_map),
                  pl.BlockSpec((1, D_MODEL, EXPERT_FF), w_map),
                  pl.BlockSpec((1, EXPERT_FF, D_MODEL), w_map)],
        out_specs=pl.BlockSpec((EXPERT_TILE * ROW_TILE_ROWS, LANES), lambda i, te, nu, pb, st: (i, 0)),
        scratch_shapes=[pltpu.VMEM((GATHER_SLOTS, EXPERT_TILE * ROW_TILE_ROWS, LANES), jnp.float32),
                        pltpu.SemaphoreType.DMA((GATHER_SLOTS,)),
                        pltpu.VMEM((D_MODEL, EXPERT_FF), jnp.bfloat16),
                        pltpu.VMEM((D_MODEL, EXPERT_FF), jnp.bfloat16),
                        pltpu.VMEM((EXPERT_FF, D_MODEL), jnp.bfloat16)],
    )
    return pl.pallas_call(
        _expert_kernel,
        grid_spec=grid_spec,
        out_shape=jax.ShapeDtypeStruct((n_tiles * EXPERT_TILE * ROW_TILE_ROWS, LANES), jnp.float32),
        compiler_params=pltpu.CompilerParams(dimension_semantics=("arbitrary",),
                                             vmem_limit_bytes=VMEM_LIMIT),
        name="experts",
    )(tile_expert, n_used, tile_pair_base, sorted_token, h1, wg, wu, wd)


def _combine_kernel(pos_ref, h1_ref, route_ref, pp_ref, ps_ref, ys_hbm, w_ple_proj_ref, w_ple_gate_ref,
                    ln2_g_ref, ln2_b_ref, yp_ref, ys_out_ref, ybuf, sems, *, n_prompt_tiles):
    i = pl.program_id(0)
    n_steps = pl.num_programs(0)
    bf16 = jnp.bfloat16
    slot = lax.rem(i, 2)

    def start_gather(tile, dst_slot):
        def issue(r, carry):
            pair = 2 * (tile * TOKEN_TILE + r)
            dst = pl.multiple_of(r * ROW_TILE_ROWS, ROW_TILE_ROWS)
            for which in range(2):
                src = pl.multiple_of(pos_ref[pair + which] * ROW_TILE_ROWS, ROW_TILE_ROWS)
                pltpu.make_async_copy(ys_hbm.at[pl.ds(src, ROW_TILE_ROWS)],
                                      ybuf.at[dst_slot, which, pl.ds(dst, ROW_TILE_ROWS)],
                                      sems.at[dst_slot]).start(priority=which)
            return carry
        lax.fori_loop(0, TOKEN_TILE, issue, 0, unroll=4)

    def wait_gather(dst_slot):
        for which in range(2):
            pltpu.make_async_copy(ys_hbm.at[pl.ds(0, TOKEN_TILE * ROW_TILE_ROWS)], ybuf.at[dst_slot, which],
                                  sems.at[dst_slot]).wait()

    @pl.when(i == 0)
    def _():
        start_gather(0, 0)

    @pl.when(i + 1 < n_steps)
    def _():
        start_gather(i + 1, 1 - slot)

    h1 = _load_row_tiles(h1_ref, TOKEN_TILE)
    is_prompt = i < n_prompt_tiles
    p = jnp.where(is_prompt, pp_ref[...], ps_ref[...])
    pe = _dot(p.astype(bf16), w_ple_proj_ref[...]) * _sigmoid(_dot(h1.astype(bf16), w_ple_gate_ref[...]))
    route = route_ref[...]
    w1 = route[:, 2:3]
    w2 = route[:, 3:4]
    wait_gather(slot)
    y1 = _load_row_tiles(ybuf.at[slot, 0], TOKEN_TILE)
    y2 = _load_row_tiles(ybuf.at[slot, 1], TOKEN_TILE)
    total = DN_ALPHA * h1 + pe + w1 * y1 + w2 * y2
    out = _layer_norm(total, ln2_g_ref[...], ln2_b_ref[...])

    @pl.when(is_prompt)
    def _():
        yp_ref[...] = out

    @pl.when(jnp.logical_not(is_prompt))
    def _():
        ys_out_ref[...] = out


def _combine_call(pos, h1, route, p_prompt, p_sample, ys, w_ple_proj, w_ple_gate, ln2_g, ln2_b):
    n_tok = route.shape[0]
    n_prompt_tiles = p_prompt.shape[0] // TOKEN_TILE
    n_sample_tiles = p_sample.shape[0] // TOKEN_TILE
    n_steps = n_tok // TOKEN_TILE
    assert n_steps == n_prompt_tiles + n_sample_tiles
    tok_spec = lambda cols: pl.BlockSpec((TOKEN_TILE, cols), lambda i, pos: (i, 0))
    prompt_spec = lambda cols: pl.BlockSpec((TOKEN_TILE, cols),
                                            lambda i, pos: (jnp.minimum(i, n_prompt_tiles - 1), 0))
    sample_spec = lambda cols: pl.BlockSpec((TOKEN_TILE, cols),
                                            lambda i, pos: (jnp.maximum(i - n_prompt_tiles, 0), 0))
    const_spec = lambda shape: pl.BlockSpec(shape, lambda i, pos: (0, 0))
    grid_spec = pltpu.PrefetchScalarGridSpec(
        num_scalar_prefetch=1,
        grid=(n_steps,),
        in_specs=[pl.BlockSpec((TOKEN_TILE * ROW_TILE_ROWS, LANES), lambda i, pos: (i, 0)), tok_spec(LANES),
                  prompt_spec(PLE_DIM), sample_spec(PLE_DIM),
                  pl.BlockSpec(memory_space=pl.ANY), const_spec(w_ple_proj.shape), const_spec(w_ple_gate.shape),
                  const_spec(ln2_g.shape), const_spec(ln2_b.shape)],
        out_specs=(prompt_spec(D_MODEL), sample_spec(D_MODEL)),
        scratch_shapes=[pltpu.VMEM((2, 2, TOKEN_TILE * ROW_TILE_ROWS, LANES), jnp.float32),
                        pltpu.SemaphoreType.DMA((2,))],
    )
    return pl.pallas_call(
        functools.partial(_combine_kernel, n_prompt_tiles=n_prompt_tiles),
        grid_spec=grid_spec,
        out_shape=(jax.ShapeDtypeStruct((n_prompt_tiles * TOKEN_TILE, D_MODEL), jnp.float32),
                   jax.ShapeDtypeStruct((n_sample_tiles * TOKEN_TILE, D_MODEL), jnp.float32)),
        compiler_params=pltpu.CompilerParams(dimension_semantics=("arbitrary",),
                                             vmem_limit_bytes=VMEM_LIMIT),
        name="combine",
    )(pos, h1, route, p_prompt, p_sample, ys, w_ple_proj, w_ple_gate, ln2_g, ln2_b)


def _dispatch_plan(route, n_tiles):
    f32 = jnp.float32
    exact = lax.Precision.HIGHEST
    experts = route[:, :2].astype(jnp.int32).reshape(-1)
    n_pairs = experts.shape[0]
    expert_ids = jnp.arange(N_EXPERTS, dtype=jnp.int32)
    onehot = (experts[None, :] == expert_ids[:, None]).astype(jnp.int32)
    running = jnp.cumsum(onehot, axis=1)
    counts = running[:, -1].astype(f32)
    tiles_per = jnp.floor((counts + (EXPERT_TILE - 1)) / EXPERT_TILE)

    upper = (expert_ids[:, None] <= expert_ids[None, :]).astype(f32)
    ends = jnp.dot(jnp.stack([tiles_per, counts]), upper, precision=exact)
    tile_end, pair_end = ends[0], ends[1]
    tile_start = tile_end - tiles_per
    row_start = tile_start * EXPERT_TILE
    pos = jnp.sum(onehot * (running - 1 + row_start.astype(jnp.int32)[:, None]), axis=0)
    n_used = tile_end[-1]
    tile_index = jnp.arange(n_tiles, dtype=jnp.int32).astype(f32)
    clamped = jnp.minimum(tile_index, n_used - 1)[:, None]
    member = ((clamped >= tile_start[None, :]) & (clamped < tile_end[None, :])).astype(f32)
    table = jnp.stack([expert_ids.astype(f32), pair_end - counts - row_start], axis=1)
    per_tile = jnp.dot(member, table, precision=exact)
    tile_expert = per_tile[:, 0]
    tile_pair_base = jnp.clip(per_tile[:, 1] + tile_index * EXPERT_TILE, 0, n_pairs)

    assert n_pairs <= 1 << PAIR_BITS
    keys = jnp.sort(experts * (1 << PAIR_BITS) + jnp.arange(n_pairs, dtype=jnp.int32))
    sorted_token = jnp.concatenate([(keys & ((1 << PAIR_BITS) - 1)) >> 1,
                                    jnp.zeros((EXPERT_TILE,), jnp.int32)])
    return (pos.astype(jnp.int32), sorted_token, tile_expert.astype(jnp.int32),
            tile_pair_base.astype(jnp.int32), n_used.reshape(1).astype(jnp.int32))


def kernel(x_prompt, x_sample, p_prompt, p_sample, state_ret, state_conv, ln_in_g, ln_in_b, w_in, w_ret_out,
           conv_w, conv_b, conv_ln_g, conv_ln_b, w_conv_out, w_out, ln1_g, ln1_b, w_route_g, b_route_g,
           w_route_e, b_route_e, w_exp_gate, w_exp_up, w_exp_down, w_ple_proj, w_ple_gate, ln2_g, ln2_b):
    assert w_in.shape[0] == DEPTH == 1
    bf16 = jnp.bfloat16
    batch, seq, _ = x_prompt.shape
    dec_batch, dec_seq, _ = x_sample.shape
    assert dec_seq == CHUNK and seq % TOKEN_TILE == 0 and TOKEN_TILE % CHUNK == 0
    row = lambda a: a.reshape(1, -1)
    n_prompt = batch * seq
    n_sample = dec_batch * dec_seq
    n_tok = n_prompt + n_sample

    w_route = jnp.concatenate(
        [w_route_g[0], jnp.transpose(w_route_e[0], (1, 0, 2)).reshape(D_MODEL, N_EXPERTS),
         jnp.zeros((D_MODEL, LANES - ROUTE_COLS), jnp.float32)], axis=1)
    w_route_hi = w_route.astype(bf16)
    w_route_lo = (w_route - w_route_hi.astype(jnp.float32)).astype(bf16)
    b_route = jnp.concatenate([b_route_g[0], b_route_e[0].reshape(-1),
                               jnp.zeros((LANES - ROUTE_COLS,), jnp.float32)]).reshape(1, LANES)
    conv_w_pad = jnp.concatenate([conv_w[0], jnp.zeros((HIST_ROWS - CONV_WIDTH, D_MODEL), jnp.float32)], axis=0)
    weights = (row(ln_in_g), row(ln_in_b), w_in[0].astype(bf16), w_ret_out[0].astype(bf16), conv_w_pad,
               row(conv_b[0]), row(conv_ln_g[0]), row(conv_ln_b[0]), w_conv_out[0].astype(bf16),
               w_out[0].astype(bf16), row(ln1_g[0]), row(ln1_b[0]),
               jnp.concatenate([w_route_hi, w_route_lo], axis=1), w_route_hi, b_route)

    steps_p = seq // TOKEN_TILE
    sample_tiles = n_sample // TOKEN_TILE
    assert sample_tiles % steps_p == 0
    tables_p = _mix_tables(np.arange(seq), 1, TOKEN_TILE)
    rinit_p = jnp.zeros((batch, RET_HEADS, RET_DK, RET_DV), jnp.float32)
    cinit_p = jnp.zeros((batch, HIST_ROWS, D_MODEL), jnp.float32)
    h1, route, rnew_p, cnew_p = _mix_call(
        x_prompt.reshape(n_prompt, D_MODEL), rinit_p, cinit_p, tables_p, weights, None, n_tok_total=n_tok,
        n_seq=1, seq_rows=TOKEN_TILE, n_groups=batch, n_steps=steps_p, table_per_step=True,
        tile_offset=0, n_fill_groups=sample_tiles // steps_p)

    seqs_per_tile = TOKEN_TILE // dec_seq
    assert dec_batch % seqs_per_tile == 0
    pos_s = np.tile(PAST_LEN + np.arange(dec_seq), seqs_per_tile)
    tables_s = _mix_tables(pos_s, seqs_per_tile, dec_seq)
    cinit_s = jnp.pad(state_conv[0], ((0, 0), (HIST_PAD, 0), (0, 0)))
    h1, route, rnew_s, cnew_s = _mix_call(
        x_sample.reshape(n_sample, D_MODEL), state_ret[0], cinit_s, tables_s, weights, (h1, route),
        n_tok_total=n_tok, n_seq=seqs_per_tile, seq_rows=dec_seq, n_groups=dec_batch // seqs_per_tile,
        n_steps=1, table_per_step=False, tile_offset=n_prompt // TOKEN_TILE, n_fill_groups=0)

    route = _route_call(route)
    n_tiles = (2 * n_tok) // EXPERT_TILE + N_EXPERTS
    pos, sorted_token, tile_expert, tile_pair_base, n_used = _dispatch_plan(route, n_tiles)
    ys = _expert_call(tile_expert, n_used, tile_pair_base, sorted_token, h1,
                      w_exp_gate[0].reshape(N_EXPERTS, D_MODEL, EXPERT_FF),
                      w_exp_up[0].reshape(N_EXPERTS, D_MODEL, EXPERT_FF),
                      w_exp_down[0].reshape(N_EXPERTS, EXPERT_FF, D_MODEL))
    y_p, y_s = _combine_call(pos, h1, route, p_prompt[0].reshape(n_prompt, PLE_DIM),
                             p_sample[0].reshape(n_sample, PLE_DIM), ys, w_ple_proj[0].astype(bf16),
                             w_ple_gate[0].astype(bf16), row(ln2_g[0]), row(ln2_b[0]))

    return (y_p.reshape(batch, seq, D_MODEL), y_s.reshape(dec_batch, dec_seq, D_MODEL), rnew_p[None],
            cnew_p[None, :, HIST_PAD:, :], rnew_s[None], cnew_s[None, :, HIST_PAD:, :])
```

```python
import functools
import math

import jax
import jax.numpy as jnp
import numpy as np
from jax import lax
from jax.experimental import pallas as pl
from jax.experimental.pallas import tpu as pltpu

D_MODEL = 1024
RET_HEADS = 4
RET_DK = 128
RET_DV = 256
RET_QK = RET_HEADS * RET_DK
RET_V = RET_HEADS * RET_DV
CHUNK = 64
CONV_WIDTH = 31
CONV_HIST = CONV_WIDTH - 1
HIST_ROWS = 32
HIST_PAD = HIST_ROWS - CONV_HIST
N_GROUPS = 4
EXPERTS_PER_GROUP = 8
N_EXPERTS = N_GROUPS * EXPERTS_PER_GROUP
EXPERT_FF = 512
PLE_DIM = 256
PAST_LEN = 4096
LN_EPS = 1e-5
ROPE_BASE = 10000.0
DEPTH = 1
DN_ALPHA = float((2 * DEPTH) ** 0.25)

LANES = 128
TOKEN_TILE = 256
EXPERT_TILE = 256
COMBINE_SECTION = 128
ROUTE_TILE = 2048
GATHER_SLOTS = 3
LANE_CHUNKS = D_MODEL // LANES
ROW_TILE_ROWS = D_MODEL // LANES
CONV_ROW_BLOCK = 64
VMEM_LIMIT = 56 * 1024 * 1024

C_Q = 0
C_K = RET_QK
C_V = 2 * RET_QK
C_G = C_V + RET_V
C_GLU_A = C_G + RET_V
C_GLU_B = C_GLU_A + D_MODEL
C_GA = C_GLU_B + D_MODEL
C_GB = C_GA + D_MODEL

ROUTE_COLS = N_GROUPS + N_EXPERTS
PAIR_BITS = 16
NEG_BIG = -1e30
NEG_LOG2_E = -math.log2(math.e)


def _sigmoid(x):
    return 1.0 / (1.0 + jnp.exp2(x * NEG_LOG2_E))


def _normalize(x):
    mu = jnp.mean(x, axis=-1, keepdims=True)
    xc = x - mu
    var = jnp.mean(xc * xc, axis=-1, keepdims=True)
    return xc * lax.rsqrt(var + LN_EPS)


def _layer_norm(x, g, b):
    return _normalize(x) * g + b


def _dot(a, b):
    return jnp.dot(a, b, preferred_element_type=jnp.float32)


def _store_row_tiles(ref, x):
    n = x.shape[0]
    for j in range(ROW_TILE_ROWS):
        ref[pl.ds(j, n, stride=ROW_TILE_ROWS), :] = x[:, j * LANES:(j + 1) * LANES]


def _load_row_tiles(ref, n, first=0):
    return jnp.concatenate([ref[pl.ds(first * ROW_TILE_ROWS + j, n, stride=ROW_TILE_ROWS), :]
                            for j in range(ROW_TILE_ROWS)], axis=1)


N_MIX_INPUTS = 25


def _mix_kernel(*refs, n_real_groups, n_fill_groups, aliased, **tile_params):
    if aliased:
        refs = refs[:N_MIX_INPUTS] + refs[N_MIX_INPUTS + 2:]
    h1_ref, route_ref = refs[N_MIX_INPUTS], refs[N_MIX_INPUTS + 1]
    if n_fill_groups == 0:
        _mix_tile(*refs, **tile_params)
        return
    group = pl.program_id(0)

    @pl.when(group < n_real_groups)
    def _():
        _mix_tile(*refs, **tile_params)

    @pl.when(group >= n_real_groups)
    def _():
        h1_ref[...] = jnp.zeros_like(h1_ref)
        route_ref[...] = jnp.zeros_like(route_ref)


def _mix_tile(x_ref, rinit_ref, cinit_ref, cq_ref, sq_ref, ck_ref, sk_ref,
              xi_ref, zeta_ref, mask_ref,
              ln_in_g_ref, ln_in_b_ref, w_in_ref, w_ret_out_ref, conv_w_ref, conv_b_ref,
              conv_ln_g_ref, conv_ln_b_ref, w_conv_out_ref, w_out_ref, ln1_g_ref, ln1_b_ref,
              w_route_ref, w_route_hi_ref, b_route_ref,
              h1_ref, route_ref, rnew_ref, cnew_ref,
              r_scr, ubuf, c_scr,
              *, n_seq, seq_rows, state_decay):
    t = pl.program_id(1)
    last_t = pl.num_programs(1) - 1
    bf16 = jnp.bfloat16

    @pl.when(t == 0)
    def _():
        r_scr[...] = rinit_ref[...]
        for j in range(LANE_CHUNKS):
            ubuf[:, j, 0:HIST_ROWS, :] = cinit_ref[:, :, j * LANES:(j + 1) * LANES]

    h = _layer_norm(x_ref[...], ln_in_g_ref[...], ln_in_b_ref[...])
    hb = h.astype(bf16)

    u = _dot(hb, w_in_ref[:, C_GLU_A:C_GLU_A + D_MODEL]) * _sigmoid(
        _dot(hb, w_in_ref[:, C_GLU_B:C_GLU_B + D_MODEL]))
    for sq_i in range(n_seq):
        for j in range(LANE_CHUNKS):
            ubuf[sq_i, j, HIST_ROWS:HIST_ROWS + seq_rows, :] = u[sq_i * seq_rows:(sq_i + 1) * seq_rows,
                                                                 j * LANES:(j + 1) * LANES]

    row_block = min(CONV_ROW_BLOCK, seq_rows)

    def conv_lane_chunk(j):
        lane = slice(j * LANES, (j + 1) * LANES)
        for sq_i in range(n_seq):
            for rb in range(seq_rows // row_block):
                r0 = rb * row_block + HIST_PAD
                acc = None
                for kk in range(CONV_WIDTH):
                    term = conv_w_ref[kk:kk + 1, lane] * ubuf[sq_i, j, r0 + kk:r0 + kk + row_block, :]
                    acc = term if acc is None else acc + term
                out_row = sq_i * seq_rows + rb * row_block
                c_scr[out_row:out_row + row_block, lane] = acc

    assert LANE_CHUNKS == 2 * RET_HEADS

    q = _dot(hb, w_in_ref[:, C_Q:C_Q + RET_QK])
    conv_lane_chunk(0)
    k = _dot(hb, w_in_ref[:, C_K:C_K + RET_QK])
    conv_lane_chunk(1)
    v = _dot(hb, w_in_ref[:, C_V:C_V + RET_V])
    conv_lane_chunk(2)
    g = _dot(hb, w_in_ref[:, C_G:C_G + RET_V])
    g = g * _sigmoid(g)
    ga = _sigmoid(_dot(hb, w_in_ref[:, C_GA:C_GA + D_MODEL]))
    conv_lane_chunk(3)
    gb = _sigmoid(_dot(hb, w_in_ref[:, C_GB:C_GB + D_MODEL]))
    cq, sq, ck, sk = cq_ref[...], sq_ref[...], ck_ref[...], sk_ref[...]

    y_ret = None
    for hd in range(RET_HEADS):
        qh = q[:, hd * RET_DK:(hd + 1) * RET_DK]
        kh = k[:, hd * RET_DK:(hd + 1) * RET_DK]
        qr = qh * cq + pltpu.roll(qh, RET_DK // 2, 1) * sq
        kr = kh * ck + pltpu.roll(kh, RET_DK // 2, 1) * sk
        vb = v[:, hd * RET_DV:(hd + 1) * RET_DV].astype(bf16)
        s = lax.dot_general(qr.astype(bf16), kr.astype(bf16), (((1,), (1,)), ((), ())),
                            preferred_element_type=jnp.float32)
        inner = _dot((s * mask_ref[hd]).astype(bf16), vb)
        qx = (qr * xi_ref[hd]).astype(bf16)
        kz = (kr * zeta_ref[hd]).astype(bf16)
        o_parts = []
        for sq_i in range(n_seq):
            rows = slice(sq_i * seq_rows, (sq_i + 1) * seq_rows)
            r_old = r_scr[sq_i, hd]
            o_parts.append(inner[rows] + _dot(qx[rows], r_old.astype(bf16)))
            kv = lax.dot_general(kz[rows], vb[rows], (((0,), (0,)), ((), ())),
                                 preferred_element_type=jnp.float32)
            r_scr[sq_i, hd] = state_decay[hd] * r_old + kv
        o = o_parts[0] if n_seq == 1 else jnp.concatenate(o_parts, axis=0)
        on = _normalize(o)
        gated = (on * g[:, hd * RET_DV:(hd + 1) * RET_DV]).astype(bf16)
        part = _dot(gated, w_ret_out_ref[hd * RET_DV:(hd + 1) * RET_DV, :])
        y_ret = part if y_ret is None else y_ret + part
        conv_lane_chunk(RET_HEADS + hd)

    for sq_i in range(n_seq):
        for j in range(LANE_CHUNKS):
            ubuf[sq_i, j, 0:HIST_ROWS, :] = ubuf[sq_i, j, seq_rows:seq_rows + HIST_ROWS, :]

    c = _layer_norm(c_scr[...] + conv_b_ref[...], conv_ln_g_ref[...], conv_ln_b_ref[...])
    c = c * _sigmoid(c)
    y_conv = _dot(c.astype(bf16), w_conv_out_ref[...])

    merged =(ga * y_ret + gb * y_conv).astype(bf16)
    h1 = _layer_norm(DN_ALPHA * h + _dot(merged, w_out_ref[...]), ln1_g_ref[...], ln1_b_ref[...])
    _store_row_tiles(h1_ref, h1)

    h1_hi = h1.astype(bf16)
    h1_lo = (h1 - h1_hi.astype(jnp.float32)).astype(bf16)
    both = _dot(h1_hi, w_route_ref[...])
    route_ref[...] = both[:, :LANES] + both[:, LANES:] + _dot(h1_lo, w_route_hi_ref[...]) + b_route_ref[...]

    @pl.when(t == last_t)
    def _():
        rnew_ref[...] = r_scr[...]
        for j in range(LANE_CHUNKS):
            cnew_ref[:, :, j * LANES:(j + 1) * LANES] = ubuf[:, j, 0:HIST_ROWS, :]


def _route_kernel(logits_ref, route_ref):
    logits = logits_ref[...]
    lane = lax.broadcasted_iota(jnp.int32, logits.shape, 1)
    lane_f = lane.astype(jnp.float32)
    is_group = lane < N_GROUPS
    gl = jnp.where(is_group, logits, NEG_BIG)
    gmax = jnp.max(gl, axis=-1, keepdims=True)
    gexp = jnp.where(is_group, jnp.exp(gl - gmax), 0.0)
    gprob = gexp / jnp.sum(gexp, axis=-1, keepdims=True)
    gp = jnp.max(gprob, axis=-1, keepdims=True)
    gi = jnp.min(jnp.where(is_group & (gprob == gp), lane_f, float(LANES)), axis=-1, keepdims=True)
    e_lo = N_GROUPS + EXPERTS_PER_GROUP * gi
    in_group = (lane_f >= e_lo) & (lane_f < e_lo + EXPERTS_PER_GROUP)
    el = jnp.where(in_group, logits, NEG_BIG)
    ev1 = jnp.max(el, axis=-1, keepdims=True)
    ei1 = jnp.min(jnp.where(in_group & (el == ev1), lane_f, float(LANES)), axis=-1, keepdims=True)
    rest = in_group & (lane_f != ei1)
    el2 = jnp.where(rest, logits, NEG_BIG)
    ev2 = jnp.max(el2, axis=-1, keepdims=True)
    ei2 = jnp.min(jnp.where(rest & (el2 == ev2), lane_f, float(LANES)), axis=-1, keepdims=True)
    x2 = jnp.exp(ev2 - ev1)
    w1 = gp / (1.0 + x2)
    w2 = w1 * x2
    route = jnp.where(lane == 0, ei1 - N_GROUPS, 0.0)
    route = jnp.where(lane == 1, ei2 - N_GROUPS, route)
    route = jnp.where(lane == 2, w1, route)
    route = jnp.where(lane == 3, w2, route)
    route_ref[...] = route


def _route_call(logits):
    n_tok = logits.shape[0]
    assert n_tok % ROUTE_TILE == 0
    spec = pl.BlockSpec((ROUTE_TILE, LANES), lambda i: (i, 0))
    return pl.pallas_call(
        _route_kernel,
        grid=(n_tok // ROUTE_TILE,),
        in_specs=[spec],
        out_specs=spec,
        out_shape=jax.ShapeDtypeStruct(logits.shape, jnp.float32),
        compiler_params=pltpu.CompilerParams(dimension_semantics=("arbitrary",)),
        name="route",
    )(logits)


def _const_spec(shape):
    zeros = (0,) * len(shape)
    return pl.BlockSpec(shape, lambda b, t: zeros, pipeline_mode=pl.Buffered(1))


def _mix_call(x, rinit, cinit, tables, weights, shared, *, n_tok_total, n_seq, seq_rows, n_groups, n_steps,
              table_per_step, tile_offset, n_fill_groups):
    tt = n_seq * seq_rows
    assert tt == TOKEN_TILE and x.shape[0] == n_groups * n_steps * tt
    log_decay = [math.log1p(-(2.0 ** (-5.0 - hd))) for hd in range(RET_HEADS)]
    state_decay = tuple(math.exp(lg * seq_rows) for lg in log_decay)
    cq, sq, ck, sk, xi, zeta, mask = tables
    last_group = n_groups - 1
    own = lambda b: jnp.minimum(b, last_group)

    in_tok_spec = pl.BlockSpec((tt, D_MODEL), lambda b, t: (own(b) * n_steps + t, 0))
    out_tok_spec = lambda rows: pl.BlockSpec((rows, LANES), lambda b, t: (tile_offset + b * n_steps + t, 0))
    rope_spec = pl.BlockSpec((tt, LANES), (lambda b, t: (t, 0)) if table_per_step else (lambda b, t: (0, 0)))
    state_r_spec = pl.BlockSpec((n_seq, RET_HEADS, RET_DK, RET_DV), lambda b, t: (own(b), 0, 0, 0))
    state_c_spec = pl.BlockSpec((n_seq, HIST_ROWS, D_MODEL), lambda b, t: (own(b), 0, 0))
    any_spec = pl.BlockSpec(memory_space=pl.ANY)

    in_specs = [in_tok_spec, state_r_spec, state_c_spec, rope_spec, rope_spec, rope_spec, rope_spec,
                _const_spec(xi.shape), _const_spec(zeta.shape), _const_spec(mask.shape)]
    in_specs += [_const_spec(w.shape) for w in weights]
    assert len(in_specs) == N_MIX_INPUTS
    aliases = {}
    extra = ()
    if shared is not None:
        assert n_fill_groups == 0
        in_specs += [any_spec, any_spec]
        aliases = {N_MIX_INPUTS: 0, N_MIX_INPUTS + 1: 1}
        extra = tuple(shared)
    out_shape = (jax.ShapeDtypeStruct((n_tok_total * ROW_TILE_ROWS, LANES), jnp.float32),
                 jax.ShapeDtypeStruct((n_tok_total, LANES), jnp.float32),
                 jax.ShapeDtypeStruct(rinit.shape, jnp.float32),
                 jax.ShapeDtypeStruct(cinit.shape, jnp.float32))
    out_specs = (out_tok_spec(tt * ROW_TILE_ROWS), out_tok_spec(tt), state_r_spec, state_c_spec)
    body = functools.partial(_mix_kernel, n_real_groups=n_groups, n_fill_groups=n_fill_groups,
                             aliased=shared is not None,
                             n_seq=n_seq, seq_rows=seq_rows, state_decay=state_decay)
    return pl.pallas_call(
        body,
        grid=(n_groups + n_fill_groups, n_steps),
        in_specs=in_specs,
        out_specs=out_specs,
        out_shape=out_shape,
        input_output_aliases=aliases,
        scratch_shapes=[pltpu.VMEM((n_seq, RET_HEADS, RET_DK, RET_DV), jnp.float32),
                        pltpu.VMEM((n_seq, LANE_CHUNKS, HIST_ROWS + seq_rows, LANES), jnp.float32),
                        pltpu.VMEM((tt, D_MODEL), jnp.float32)],
        compiler_params=pltpu.CompilerParams(dimension_semantics=("arbitrary", "arbitrary"),
                                             vmem_limit_bytes=VMEM_LIMIT),
        name="mix",
    )(x, rinit, cinit, cq, sq, ck, sk, xi, zeta, mask, *weights, *extra)


def _mix_tables(positions, n_seq, seq_rows):
    half = RET_DK // 2
    inv = ROPE_BASE ** (-np.arange(half, dtype=np.float64) / half)
    ang = np.asarray(positions, np.float64)[:, None] * inv[None, :]
    cos, sin = np.cos(ang), np.sin(ang)
    cq = np.concatenate([cos, cos], axis=1)
    sq = np.concatenate([-sin, sin], axis=1)
    scale = RET_DK ** -0.5
    lg = np.log1p(-np.exp2(-5.0 - np.arange(RET_HEADS, dtype=np.float64)))
    tt = n_seq * seq_rows
    i = np.arange(tt)
    loc = (i % seq_rows).astype(np.float64)
    xi = np.exp(lg[:, None] * (loc + 1.0)[None, :])
    zeta = np.exp(lg[:, None] * (seq_rows - 1.0 - loc)[None, :])
    xi = np.broadcast_to(xi[:, :, None], (RET_HEADS, tt, LANES))
    zeta = np.broadcast_to(zeta[:, :, None], (RET_HEADS, tt, LANES))
    same_seq = (i[:, None] // seq_rows) == (i[None, :] // seq_rows)
    visible = same_seq & ((i[None, :] // CHUNK) <= (i[:, None] // CHUNK))
    dist = np.abs(i[:, None] - i[None, :]).astype(np.float64)
    mask = np.where(visible[None], np.exp(lg[:, None, None] * dist[None]), 0.0)
    tables = (cq, sq, cq * scale, sq * scale, xi, zeta, mask)
    return tuple(jnp.asarray(np.ascontiguousarray(tab), jnp.float32) for tab in tables)


def _expert_kernel(tile_expert_ref, n_used_ref, tile_pair_base_ref, sorted_token_ref, h1_hbm, wg_ref, wu_ref,
                   wd_ref, ys_ref, xbuf, sems, wg_b, wu_b, wd_b):
    i = pl.program_id(0)
    n_used = n_used_ref[0]
    bf16 = jnp.bfloat16
    slot = lax.rem(i, GATHER_SLOTS)

    def start_gather(tile):
        dst_slot = lax.rem(tile, GATHER_SLOTS)
        pair_base = tile_pair_base_ref[tile]

        def issue(r, carry):
            src = pl.multiple_of(sorted_token_ref[pair_base + r] * ROW_TILE_ROWS, ROW_TILE_ROWS)
            dst = pl.multiple_of(r * ROW_TILE_ROWS, ROW_TILE_ROWS)
            pltpu.make_async_copy(h1_hbm.at[pl.ds(src, ROW_TILE_ROWS)],
                                  xbuf.at[dst_slot, pl.ds(dst, ROW_TILE_ROWS)],
                                  sems.at[dst_slot]).start(priority=1)
            return carry
        lax.fori_loop(0, EXPERT_TILE, issue, 0, unroll=8)

    def wait_gather(dst_slot):
        pltpu.make_async_copy(h1_hbm.at[pl.ds(0, EXPERT_TILE * ROW_TILE_ROWS)], xbuf.at[dst_slot],
                              sems.at[dst_slot]).wait()

    for first in range(GATHER_SLOTS - 1):
        @pl.when((i == 0) & (first < n_used))
        def _(first=first):
            start_gather(first)

    @pl.when(i + GATHER_SLOTS - 1 < n_used)
    def _():
        start_gather(i + GATHER_SLOTS - 1)

    prev = tile_expert_ref[jnp.maximum(i - 1, 0)]
    new_expert = (i == 0) | (tile_expert_ref[i] != prev)

    @pl.when(new_expert)
    def _():
        wg_b[...] = wg_ref[0].astype(bf16)
        wu_b[...] = wu_ref[0].astype(bf16)
        wd_b[...] = wd_ref[0].astype(bf16)

    @pl.when(i < n_used)
    def _():
        wait_gather(slot)
        x = _load_row_tiles(xbuf.at[slot], EXPERT_TILE).astype(bf16)
        gate = _dot(x, wg_b[...])
        up = _dot(x, wu_b[...])
        act = (gate * _sigmoid(gate) * up).astype(bf16)
        _store_row_tiles(ys_ref, _dot(act, wd_b[...]))

    @pl.when(i >= n_used)
    def _():
        ys_ref[...] = jnp.zeros_like(ys_ref)


def _expert_call(tile_expert, n_used, tile_pair_base, sorted_token, h1, wg, wu, wd):
    n_tiles = tile_expert.shape[0]
    w_map = lambda i, te, nu, pb, st: (te[i], 0, 0)
    grid_spec = pltpu.PrefetchScalarGridSpec(
        num_scalar_prefetch=4,
        grid=(n_tiles,),
        in_specs=[pl.BlockSpec(memory_space=pl.ANY),
                  pl.BlockSpec((1, D_MODEL, EXPERT_FF), w_map),
                  pl.BlockSpec((1, D_MODEL, EXPERT_FF), w_map),
                  pl.BlockSpec((1, EXPERT_FF, D_MODEL), w_map)],
        out_specs=pl.BlockSpec((EXPERT_TILE * ROW_TILE_ROWS, LANES), lambda i, te, nu, pb, st: (i, 0)),
        scratch_shapes=[pltpu.VMEM((GATHER_SLOTS, EXPERT_TILE * ROW_TILE_ROWS, LANES), jnp.float32),
                        pltpu.SemaphoreType.DMA((GATHER_SLOTS,)),
                        pltpu.VMEM((D_MODEL, EXPERT_FF), jnp.bfloat16),
                        pltpu.VMEM((D_MODEL, EXPERT_FF), jnp.bfloat16),
                        pltpu.VMEM((EXPERT_FF, D_MODEL), jnp.bfloat16)],
    )
    return pl.pallas_call(
        _expert_kernel,
        grid_spec=grid_spec,
        out_shape=jax.ShapeDtypeStruct((n_tiles * EXPERT_TILE * ROW_TILE_ROWS, LANES), jnp.float32),
        compiler_params=pltpu.CompilerParams(dimension_semantics=("arbitrary",),
                                             vmem_limit_bytes=VMEM_LIMIT),
        name="experts",
    )(tile_expert, n_used, tile_pair_base, sorted_token, h1, wg, wu, wd)


def _combine_kernel(pos_ref, h1_ref, route_ref, pp_ref, ps_ref, ys_hbm, w_ple_proj_ref, w_ple_gate_ref,
                    ln2_g_ref, ln2_b_ref, yp_ref, ys_out_ref, ybuf, sems, *, n_prompt_tiles):
    i = pl.program_id(0)
    n_steps = pl.num_programs(0)
    bf16 = jnp.bfloat16
    slot = lax.rem(i, 2)

    def start_gather(tile, dst_slot):
        def issue(r, carry):
            pair = 2 * (tile * TOKEN_TILE + r)
            dst = pl.multiple_of(r * ROW_TILE_ROWS, ROW_TILE_ROWS)
            for which in range(2):
                src = pl.multiple_of(pos_ref[pair + which] * ROW_TILE_ROWS, ROW_TILE_ROWS)
                pltpu.make_async_copy(ys_hbm.at[pl.ds(src, ROW_TILE_ROWS)],
                                      ybuf.at[dst_slot, which, pl.ds(dst, ROW_TILE_ROWS)],
                                      sems.at[dst_slot]).start(priority=which)
            return carry
        lax.fori_loop(0, TOKEN_TILE, issue, 0, unroll=4)

    def wait_gather(dst_slot):
        for which in range(2):
            pltpu.make_async_copy(ys_hbm.at[pl.ds(0, TOKEN_TILE * ROW_TILE_ROWS)], ybuf.at[dst_slot, which],
                                  sems.at[dst_slot]).wait()

    @pl.when(i == 0)
    def _():
        start_gather(0, 0)

    @pl.when(i + 1 < n_steps)
    def _():
        start_gather(i + 1, 1 - slot)

    is_prompt = i < n_prompt_tiles
    wait_gather(slot)
    for first in range(0, TOKEN_TILE, COMBINE_SECTION):
        rows = slice(first, first + COMBINE_SECTION)
        h1 = _load_row_tiles(h1_ref, COMBINE_SECTION, first)
        p = jnp.where(is_prompt, pp_ref[rows, :], ps_ref[rows, :])
        pe = _dot(p.astype(bf16), w_ple_proj_ref[...]) * _sigmoid(_dot(h1.astype(bf16), w_ple_gate_ref[...]))
        route = route_ref[rows, :]
        y1 = _load_row_tiles(ybuf.at[slot, 0], COMBINE_SECTION, first)
        y2 = _load_row_tiles(ybuf.at[slot, 1], COMBINE_SECTION, first)
        total = DN_ALPHA * h1 + pe + route[:, 2:3] * y1 + route[:, 3:4] * y2
        out = _layer_norm(total, ln2_g_ref[...], ln2_b_ref[...])

        @pl.when(is_prompt)
        def _(out=out, rows=rows):
            yp_ref[rows, :] = out

        @pl.when(jnp.logical_not(is_prompt))
        def _(out=out, rows=rows):
            ys_out_ref[rows, :] = out


def _combine_call(pos, h1, route, p_prompt, p_sample, ys, w_ple_proj, w_ple_gate, ln2_g, ln2_b):
    n_tok = route.shape[0]
    n_prompt_tiles = p_prompt.shape[0] // TOKEN_TILE
    n_sample_tiles = p_sample.shape[0] // TOKEN_TILE
    n_steps = n_tok // TOKEN_TILE
    assert n_steps == n_prompt_tiles + n_sample_tiles
    tok_spec = lambda cols: pl.BlockSpec((TOKEN_TILE, cols), lambda i, pos: (i, 0))
    prompt_spec = lambda cols: pl.BlockSpec((TOKEN_TILE, cols),
                                            lambda i, pos: (jnp.minimum(i, n_prompt_tiles - 1), 0))
    sample_spec = lambda cols: pl.BlockSpec((TOKEN_TILE, cols),
                                            lambda i, pos: (jnp.maximum(i - n_prompt_tiles, 0), 0))
    const_spec = lambda shape: pl.BlockSpec(shape, lambda i, pos: (0, 0))
    grid_spec = pltpu.PrefetchScalarGridSpec(
        num_scalar_prefetch=1,
        grid=(n_steps,),
        in_specs=[pl.BlockSpec((TOKEN_TILE * ROW_TILE_ROWS, LANES), lambda i, pos: (i, 0)), tok_spec(LANES),
                  prompt_spec(PLE_DIM), sample_spec(PLE_DIM),
                  pl.BlockSpec(memory_space=pl.ANY), const_spec(w_ple_proj.shape), const_spec(w_ple_gate.shape),
                  const_spec(ln2_g.shape), const_spec(ln2_b.shape)],
        out_specs=(prompt_spec(D_MODEL), sample_spec(D_MODEL)),
        scratch_shapes=[pltpu.VMEM((2, 2, TOKEN_TILE * ROW_TILE_ROWS, LANES), jnp.float32),
                        pltpu.SemaphoreType.DMA((2,))],
    )
    return pl.pallas_call(
        functools.partial(_combine_kernel, n_prompt_tiles=n_prompt_tiles),
        grid_spec=grid_spec,
        out_shape=(jax.ShapeDtypeStruct((n_prompt_tiles * TOKEN_TILE, D_MODEL), jnp.float32),
                   jax.ShapeDtypeStruct((n_sample_tiles * TOKEN_TILE, D_MODEL), jnp.float32)),
        compiler_params=pltpu.CompilerParams(dimension_semantics=("arbitrary",),
                                             vmem_limit_bytes=VMEM_LIMIT),
        name="combine",
    )(pos, h1, route, p_prompt, p_sample, ys, w_ple_proj, w_ple_gate, ln2_g, ln2_b)


def _dispatch_plan(route, n_tiles):
    f32 = jnp.float32
    exact = lax.Precision.HIGHEST
    experts = route[:, :2].astype(jnp.int32).reshape(-1)
    n_pairs = experts.shape[0]
    expert_ids = jnp.arange(N_EXPERTS, dtype=jnp.int32)
    onehot = (experts[None, :] == expert_ids[:, None]).astype(jnp.int32)
    running = jnp.cumsum(onehot, axis=1)
    counts = running[:, -1].astype(f32)
    tiles_per = jnp.floor((counts + (EXPERT_TILE - 1)) / EXPERT_TILE)

    upper = (expert_ids[:, None] <= expert_ids[None, :]).astype(f32)
    ends = jnp.dot(jnp.stack([tiles_per, counts]), upper, precision=exact)
    tile_end, pair_end = ends[0], ends[1]
    tile_start = tile_end - tiles_per
    row_start = tile_start * EXPERT_TILE
    pos = jnp.sum(onehot * (running - 1 + row_start.astype(jnp.int32)[:, None]), axis=0)
    n_used = tile_end[-1]
    tile_index = jnp.arange(n_tiles, dtype=jnp.int32).astype(f32)
    clamped = jnp.minimum(tile_index, n_used - 1)[:, None]
    member = ((clamped >= tile_start[None, :]) & (clamped < tile_end[None, :])).astype(f32)
    table = jnp.stack([expert_ids.astype(f32), pair_end - counts - row_start], axis=1)
    per_tile = jnp.dot(member, table, precision=exact)
    tile_expert = per_tile[:, 0]
    tile_pair_base = jnp.clip(per_tile[:, 1] + tile_index * EXPERT_TILE, 0, n_pairs)

    assert n_pairs <= 1 << PAIR_BITS
    keys = jnp.sort(experts * (1 << PAIR_BITS) + jnp.arange(n_pairs, dtype=jnp.int32))
    sorted_token = jnp.concatenate([(keys & ((1 << PAIR_BITS) - 1)) >> 1,
                                    jnp.zeros((EXPERT_TILE,), jnp.int32)])
    return (pos.astype(jnp.int32), sorted_token, tile_expert.astype(jnp.int32),
            tile_pair_base.astype(jnp.int32), n_used.reshape(1).astype(jnp.int32))


def kernel(x_prompt, x_sample, p_prompt, p_sample, state_ret, state_conv, ln_in_g, ln_in_b, w_in, w_ret_out,
           conv_w, conv_b, conv_ln_g, conv_ln_b, w_conv_out, w_out, ln1_g, ln1_b, w_route_g, b_route_g,
           w_route_e, b_route_e, w_exp_gate, w_exp_up, w_exp_down, w_ple_proj, w_ple_gate, ln2_g, ln2_b):
    assert w_in.shape[0] == DEPTH == 1
    bf16 = jnp.bfloat16
    batch, seq, _ = x_prompt.shape
    dec_batch, dec_seq, _ = x_sample.shape
    assert dec_seq == CHUNK and seq % TOKEN_TILE == 0 and TOKEN_TILE % CHUNK == 0
    row = lambda a: a.reshape(1, -1)
    n_prompt = batch * seq
    n_sample = dec_batch * dec_seq
    n_tok = n_prompt + n_sample

    w_route = jnp.concatenate(
        [w_route_g[0], jnp.transpose(w_route_e[0], (1, 0, 2)).reshape(D_MODEL, N_EXPERTS),
         jnp.zeros((D_MODEL, LANES - ROUTE_COLS), jnp.float32)], axis=1)
    w_route_hi = w_route.astype(bf16)
    w_route_lo = (w_route - w_route_hi.astype(jnp.float32)).astype(bf16)
    b_route = jnp.concatenate([b_route_g[0], b_route_e[0].reshape(-1),
                               jnp.zeros((LANES - ROUTE_COLS,), jnp.float32)]).reshape(1, LANES)
    conv_w_pad = jnp.concatenate([conv_w[0], jnp.zeros((HIST_ROWS - CONV_WIDTH, D_MODEL), jnp.float32)], axis=0)
    weights = (row(ln_in_g), row(ln_in_b), w_in[0].astype(bf16), w_ret_out[0].astype(bf16), conv_w_pad,
               row(conv_b[0]), row(conv_ln_g[0]), row(conv_ln_b[0]), w_conv_out[0].astype(bf16),
               w_out[0].astype(bf16), row(ln1_g[0]), row(ln1_b[0]),
               jnp.concatenate([w_route_hi, w_route_lo], axis=1), w_route_hi, b_route)

    steps_p = seq // TOKEN_TILE
    sample_tiles = n_sample // TOKEN_TILE
    assert sample_tiles % steps_p == 0
    tables_p = _mix_tables(np.arange(seq), 1, TOKEN_TILE)
    rinit_p = jnp.zeros((batch, RET_HEADS, RET_DK, RET_DV), jnp.float32)
    cinit_p = jnp.zeros((batch, HIST_ROWS, D_MODEL), jnp.float32)
    h1, route, rnew_p, cnew_p = _mix_call(
        x_prompt.reshape(n_prompt, D_MODEL), rinit_p, cinit_p, tables_p, weights, None, n_tok_total=n_tok,
        n_seq=1, seq_rows=TOKEN_TILE, n_groups=batch, n_steps=steps_p, table_per_step=True,
        tile_offset=0, n_fill_groups=sample_tiles // steps_p)

    seqs_per_tile = TOKEN_TILE // dec_seq
    assert dec_batch % seqs_per_tile == 0
    pos_s = np.tile(PAST_LEN + np.arange(dec_seq), seqs_per_tile)
    tables_s = _mix_tables(pos_s, seqs_per_tile, dec_seq)
    cinit_s = jnp.pad(state_conv[0], ((0, 0), (HIST_PAD, 0), (0, 0)))
    h1, route, rnew_s, cnew_s = _mix_call(
        x_sample.reshape(n_sample, D_MODEL), state_ret[0], cinit_s, tables_s, weights, (h1, route),
        n_tok_total=n_tok, n_seq=seqs_per_tile, seq_rows=dec_seq, n_groups=dec_batch // seqs_per_tile,
        n_steps=1, table_per_step=False, tile_offset=n_prompt // TOKEN_TILE, n_fill_groups=0)

    route = _route_call(route)
    n_tiles = (2 * n_tok) // EXPERT_TILE + N_EXPERTS
    pos, sorted_token, tile_expert, tile_pair_base, n_used = _dispatch_plan(route, n_tiles)
    ys = _expert_call(tile_expert, n_used, tile_pair_base, sorted_token, h1,
                      w_exp_gate[0].reshape(N_EXPERTS, D_MODEL, EXPERT_FF),
                      w_exp_up[0].reshape(N_EXPERTS, D_MODEL, EXPERT_FF),
                      w_exp_down[0].reshape(N_EXPERTS, EXPERT_FF, D_MODEL))
    y_p, y_s = _combine_call(pos, h1, route, p_prompt[0].reshape(n_prompt, PLE_DIM),
                             p_sample[0].reshape(n_sample, PLE_DIM), ys, w_ple_proj[0].astype(bf16),
                             w_ple_gate[0].astype(bf16), row(ln2_g[0]), row(ln2_b[0]))

    return (y_p.reshape(batch, seq, D_MODEL), y_s.reshape(dec_batch, dec_seq, D_MODEL), rnew_p[None],
            cnew_p[None, :, HIST_PAD:, :], rnew_s[None], cnew_s[None, :, HIST_PAD:, :])
```
